```python
import math
import jax, jax.numpy as jnp
from jax import lax
import numpy as np

D_MODEL = 1024
BATCH = 8
SEQ = 2048
DEPTH = 2
DEC_BATCH = 128
DEC_SEQ = 8
PAST_LEN = 16384
PAGE_SIZE = 128

N_MIXERS = 2
N_CONV_LAYERS = (DEPTH + 1) // 2
N_SSM_LAYERS = DEPTH // 2
CONV_WIDTH = 31
GROUP_SIZE = 16
N_GROUPS = D_MODEL // GROUP_SIZE
STATE_DIM = 64
D_FF = int(math.ceil(8 * D_MODEL / 3 / 256)) * 256
EPS = 1e-6
DT_MIN = 1e-3
DT_MAX = 1e-1

kernel_name = "hybrid_conformer_conv_s5_decoder_step"


def rms_norm(x, g):
    xf = x.astype(jnp.float32)
    y = xf * lax.rsqrt(jnp.mean(xf * xf, axis=-1, keepdims=True) + EPS)
    return (y * g.astype(jnp.float32)).astype(x.dtype)


def layer_norm(x, g, b):
    xf = x.astype(jnp.float32)
    mu = jnp.mean(xf, axis=-1, keepdims=True)
    xc = xf - mu
    var = jnp.mean(xc * xc, axis=-1, keepdims=True)
    y = xc * lax.rsqrt(var + EPS) * g.astype(jnp.float32) + b.astype(jnp.float32)
    return y.astype(x.dtype)


def swiglu(h, w_gate, w_up, w_down):
    return (jax.nn.silu(h @ w_gate) * (h @ w_up)) @ w_down


def conv_module(h, buf, w_pw1, b_pw1, w_dw, b_dw, ln_g, ln_b, w_pw2):
    a = h @ w_pw1 + b_pw1
    v = a[..., :D_MODEL] * jax.nn.sigmoid(a[..., D_MODEL:])
    full = jnp.concatenate([buf.astype(v.dtype), v], axis=1)
    y = lax.conv_general_dilated(
        full, w_dw[:, None, :].astype(v.dtype), window_strides=(1,), padding='VALID',
        dimension_numbers=('NWC', 'WIO', 'NWC'), feature_group_count=D_MODEL) + b_dw
    y = jax.nn.silu(layer_norm(y, ln_g, ln_b))
    return y @ w_pw2, full[:, -(CONV_WIDTH - 1):]


def _scan_combine(e1, e2):
    a1r, a1i, b1r, b1i = e1
    a2r, a2i, b2r, b2i = e2
    return (a1r * a2r - a1i * a2i,
            a1r * a2i + a1i * a2r,
            a2r * b1r - a2i * b1i + b2r,
            a2r * b1i + a2i * b1r + b2i)


def ssm_module(h, s_re, s_im, lam_re, lam_im, log_dt, b_re, b_im, c_re, c_im, d_skip, w_glu, b_glu):
    f32 = jnp.float32
    n, l, _ = h.shape
    hf = h.astype(f32)
    u = hf.reshape(n, l, N_GROUPS, GROUP_SIZE)
    dt = jnp.exp(log_dt.astype(f32))[:, None]
    lr = lam_re.astype(f32)
    li = lam_im.astype(f32)
    mag = jnp.exp(lr * dt)
    ang = li * dt
    ab_re = mag * jnp.cos(ang)
    ab_im = mag * jnp.sin(ang)
    nr = ab_re - 1.0
    den = lr * lr + li * li
    cf_re = (nr * lr + ab_im * li) / den
    cf_im = (ab_im * lr - nr * li) / den
    bu_re = jnp.einsum('nlgc,gpc->nlgp', u, b_re.astype(f32))
    bu_im = jnp.einsum('nlgc,gpc->nlgp', u, b_im.astype(f32))
    x_re = cf_re * bu_re - cf_im * bu_im
    x_im = cf_re * bu_im + cf_im * bu_re
    h0_re = s_re.astype(f32)
    h0_im = s_im.astype(f32)
    x_re = x_re.at[:, 0].add(ab_re * h0_re - ab_im * h0_im)
    x_im = x_im.at[:, 0].add(ab_re * h0_im + ab_im * h0_re)
    a_re = jnp.broadcast_to(ab_re, (1, l, N_GROUPS, STATE_DIM))
    a_im = jnp.broadcast_to(ab_im, (1, l, N_GROUPS, STATE_DIM))
    _, _, st_re, st_im = lax.associative_scan(_scan_combine, (a_re, a_im, x_re, x_im), axis=1)
    y = (jnp.einsum('nlgp,gcp->nlgc', st_re, c_re.astype(f32))
         - jnp.einsum('nlgp,gcp->nlgc', st_im, c_im.astype(f32)))
    y = y.reshape(n, l, D_MODEL) + d_skip.astype(f32) * hf
    y = jax.nn.gelu(y).astype(h.dtype)
    z = y @ w_glu + b_glu
    out = z[..., :D_MODEL] * jax.nn.sigmoid(z[..., D_MODEL:])
    return out, st_re[:, -1], st_im[:, -1]


def setup_inputs(seed: int = 0) -> dict:
    key = jax.random.key(seed)
    ks = jax.random.split(key, 32)
    f32 = jnp.float32
    nrm = lambda k, s, sc: jax.random.normal(k, s, f32) * sc
    inv = lambda n: 1.0 / math.sqrt(n)
    n_idx = jnp.arange(STATE_DIM, dtype=f32)
    lam_re = -0.5 + nrm(ks[15], (N_SSM_LAYERS, N_GROUPS, STATE_DIM), 0.01)
    lam_im = math.pi * n_idx + nrm(ks[16], (N_SSM_LAYERS, N_GROUPS, STATE_DIM), 0.01)
    log_dt = jax.random.uniform(ks[17], (N_SSM_LAYERS, N_GROUPS), f32,
                                math.log(DT_MIN), math.log(DT_MAX))
    return {
        "x_prompt": nrm(ks[0], (BATCH, SEQ, D_MODEL), 1.0),
        "x_sample": nrm(ks[1], (DEC_BATCH, DEC_SEQ, D_MODEL), 1.0),
        "cache_conv": nrm(ks[2], (N_CONV_LAYERS, DEC_BATCH, CONV_WIDTH - 1, D_MODEL), 0.5),
        "state_ssm_re": nrm(ks[3], (N_SSM_LAYERS, DEC_BATCH, N_GROUPS, STATE_DIM), 0.1),
        "state_ssm_im": nrm(ks[4], (N_SSM_LAYERS, DEC_BATCH, N_GROUPS, STATE_DIM), 0.1),
        "norm_mix": 1.0 + nrm(ks[5], (DEPTH, D_MODEL), 0.02),
        "norm_ffn": 1.0 + nrm(ks[6], (DEPTH, D_MODEL), 0.02),
        "norm_final": 1.0 + nrm(ks[7], (D_MODEL,), 0.02),
        "conv_w_pw1": nrm(ks[8], (N_CONV_LAYERS, D_MODEL, 2 * D_MODEL), inv(D_MODEL)),
        "conv_b_pw1": nrm(ks[9], (N_CONV_LAYERS, 2 * D_MODEL), 0.01),
        "conv_w_dw": nrm(ks[10], (N_CONV_LAYERS, CONV_WIDTH, D_MODEL), inv(CONV_WIDTH)),
        "conv_b_dw": nrm(ks[11], (N_CONV_LAYERS, D_MODEL), 0.01),
        "conv_ln_g": 1.0 + nrm(ks[12], (N_CONV_LAYERS, D_MODEL), 0.02),
        "conv_ln_b": nrm(ks[13], (N_CONV_LAYERS, D_MODEL), 0.01),
        "conv_w_pw2": nrm(ks[14], (N_CONV_LAYERS, D_MODEL, D_MODEL), inv(D_MODEL)),
        "ssm_lam_re": lam_re,
        "ssm_lam_im": lam_im,
        "ssm_log_dt": log_dt,
        "ssm_b_re": nrm(ks[18], (N_SSM_LAYERS, N_GROUPS, STATE_DIM, GROUP_SIZE), inv(2 * GROUP_SIZE)),
        "ssm_b_im": nrm(ks[19], (N_SSM_LAYERS, N_GROUPS, STATE_DIM, GROUP_SIZE), inv(2 * GROUP_SIZE)),
        "ssm_c_re": nrm(ks[20], (N_SSM_LAYERS, N_GROUPS, GROUP_SIZE, STATE_DIM), inv(STATE_DIM)),
        "ssm_c_im": nrm(ks[21], (N_SSM_LAYERS, N_GROUPS, GROUP_SIZE, STATE_DIM), inv(STATE_DIM)),
        "ssm_d": 1.0 + nrm(ks[22], (N_SSM_LAYERS, D_MODEL), 0.1),
        "ssm_w_glu": nrm(ks[23], (N_SSM_LAYERS, D_MODEL, 2 * D_MODEL), inv(D_MODEL)),
        "ssm_b_glu": nrm(ks[24], (N_SSM_LAYERS, 2 * D_MODEL), 0.01),
        "ffn_w_gate": nrm(ks[25], (DEPTH, D_MODEL, D_FF), inv(D_MODEL)),
        "ffn_w_up": nrm(ks[26], (DEPTH, D_MODEL, D_FF), inv(D_MODEL)),
        "ffn_w_down": nrm(ks[27], (DEPTH, D_FF, D_MODEL), inv(D_FF)),
    }


def reference(x_prompt, x_sample, cache_conv, state_ssm_re, state_ssm_im,
              norm_mix, norm_ffn, norm_final,
              conv_w_pw1, conv_b_pw1, conv_w_dw, conv_b_dw, conv_ln_g, conv_ln_b, conv_w_pw2,
              ssm_lam_re, ssm_lam_im, ssm_log_dt, ssm_b_re, ssm_b_im, ssm_c_re, ssm_c_im,
              ssm_d, ssm_w_glu, ssm_b_glu,
              ffn_w_gate, ffn_w_up, ffn_w_down):
    yp, ys = x_prompt, x_sample
    n_p = x_prompt.shape[0]
    zero_buf = jnp.zeros((n_p, CONV_WIDTH - 1, D_MODEL), x_prompt.dtype)
    zero_state = jnp.zeros((n_p, N_GROUPS, STATE_DIM), jnp.float32)
    conv_new_p, conv_new_s = [], []
    ssm_p_re, ssm_p_im, ssm_s_re, ssm_s_im = [], [], [], []
    for i in range(DEPTH):
        j = i // N_MIXERS
        hp = rms_norm(yp, norm_mix[i])
        hs = rms_norm(ys, norm_mix[i])
        if i % N_MIXERS == 0:
            cp = (conv_w_pw1[j], conv_b_pw1[j], conv_w_dw[j], conv_b_dw[j],
                  conv_ln_g[j], conv_ln_b[j], conv_w_pw2[j])
            dp, bp = conv_module(hp, zero_buf, *cp)
            ds, bs = conv_module(hs, cache_conv[j], *cp)
            conv_new_p.append(bp)
            conv_new_s.append(bs)
        else:
            sp = (ssm_lam_re[j], ssm_lam_im[j], ssm_log_dt[j], ssm_b_re[j], ssm_b_im[j],
                  ssm_c_re[j], ssm_c_im[j], ssm_d[j], ssm_w_glu[j], ssm_b_glu[j])
            dp, pr, pi = ssm_module(hp, zero_state, zero_state, *sp)
            ds, sr, si = ssm_module(hs, state_ssm_re[j], state_ssm_im[j], *sp)
            ssm_p_re.append(pr)
            ssm_p_im.append(pi)
            ssm_s_re.append(sr)
            ssm_s_im.append(si)
        yp = yp + dp
        ys = ys + ds
        yp = yp + swiglu(rms_norm(yp, norm_ffn[i]), ffn_w_gate[i], ffn_w_up[i], ffn_w_down[i])
        ys = ys + swiglu(rms_norm(ys, norm_ffn[i]), ffn_w_gate[i], ffn_w_up[i], ffn_w_down[i])
    y_prompt = rms_norm(yp, norm_final)
    y_sample = rms_norm(ys, norm_final)
    return (y_prompt, y_sample,
            jnp.stack(conv_new_p), jnp.stack(conv_new_s),
            jnp.stack(ssm_p_re), jnp.stack(ssm_p_im),
            jnp.stack(ssm_s_re), jnp.stack(ssm_s_im))
```

```python
import functools
import math

import jax
import jax.numpy as jnp
from jax import lax
from jax.experimental import pallas as pl
from jax.experimental.pallas import tpu as pltpu

D = 1024
F = 2816
KW = 31
HIST = KW - 1
G = 64
P = 64
C = 16
EPS = 1e-6

ROWS = 1024
SSM_ROWS = 512
N_COLGRP = 4
GRP_PER_COL = G // N_COLGRP
COL_CH = D // N_COLGRP
COL_ST = GRP_PER_COL * P
FFN_CHUNKS = ((0, 768), (768, 768), (1536, 768), (2304, 512))
VMEM_LIMIT = 56 * 1024 * 1024

BF16 = jnp.bfloat16
F32 = jnp.float32


def _const_spec(shape):
    nd = len(shape)
    return pl.BlockSpec(shape, lambda i: (0,) * nd, pipeline_mode=pl.Buffered(1))


def _rms(x, g):
    return x * lax.rsqrt(jnp.mean(x * x, axis=-1, keepdims=True) + EPS) * g


def _dot(a, b):
    return jnp.dot(a, b, preferred_element_type=F32)


def _rowwise_call(body, xp, xs, consts, n_out=1, name=None):
    n_p = xp.shape[0] // ROWS
    assert xp.shape == (n_p * ROWS, D) and xs.shape == (ROWS, D)

    def kern(*refs):
        xp_ref, xs_ref = refs[0], refs[1]
        c_refs = refs[2:2 + len(consts)]
        o_refs = refs[2 + len(consts):]
        i = pl.program_id(0)

        @pl.when(i < n_p)
        def _():
            body(xp_ref, c_refs, o_refs[:n_out])

        @pl.when(i == n_p)
        def _():
            body(xs_ref, c_refs, o_refs[n_out:])

    p_spec = pl.BlockSpec((ROWS, D), lambda i: (jnp.minimum(i, n_p - 1), 0))
    s_spec = pl.BlockSpec((ROWS, D), lambda i: (0, 0))
    return pl.pallas_call(
        kern,
        grid=(n_p + 1,),
        in_specs=[p_spec, s_spec] + [_const_spec(c.shape) for c in consts],
        out_specs=[p_spec] * n_out + [s_spec] * n_out,
        out_shape=[jax.ShapeDtypeStruct(xp.shape, F32)] * n_out
        + [jax.ShapeDtypeStruct(xs.shape, F32)] * n_out,
        compiler_params=pltpu.CompilerParams(
            dimension_semantics=("arbitrary",), vmem_limit_bytes=VMEM_LIMIT),
        name=name,
    )(xp, xs, *consts)


def _pw1_body(x_ref, c_refs, o_refs):
    g_ref, w_ref, b_ref = c_refs
    h = _rms(x_ref[...], g_ref[...]).astype(BF16)
    a = _dot(h, w_ref[...]) + b_ref[...]
    o_refs[0][...] = a[:, :D] * jax.nn.sigmoid(a[:, D:])


def _ffn(x, g_ref, wg_ref, wu_ref, wd_ref):
    h = _rms(x, g_ref[...]).astype(BF16)
    acc = None
    for c0, cw in FFN_CHUNKS:
        gate = _dot(h, wg_ref[:, c0:c0 + cw])
        up = _dot(h, wu_ref[:, c0:c0 + cw])
        act = (jax.nn.silu(gate) * up).astype(BF16)
        part = _dot(act, wd_ref[c0:c0 + cw, :])
        acc = part if acc is None else acc + part
    return x + acc


def _ffn_body(x_ref, c_refs, o_refs):
    g_ref, wg_ref, wu_ref, wd_ref = c_refs
    o_refs[0][...] = _ffn(x_ref[...], g_ref, wg_ref, wu_ref, wd_ref)


def _ffn_final_body(x_ref, c_refs, o_refs):
    g_ref, wg_ref, wu_ref, wd_ref, gf_ref = c_refs
    y = _ffn(x_ref[...], g_ref, wg_ref, wu_ref, wd_ref)
    o_refs[0][...] = _rms(y, gf_ref[...])


def _depthwise_conv(full_ref, yc_ref, wdw_ref, bdw_ref, n_seq, rows):
    row_blk = 32
    for lb in range(D // 128):
        lanes = slice(lb * 128, (lb + 1) * 128)
        wk = [jnp.broadcast_to(wdw_ref[k:k + 1, lanes], (8, 128)) for k in range(KW)]
        bias = jnp.broadcast_to(bdw_ref[0:1, lanes], (8, 128))

        def body(jb, carry):
            r0 = pl.multiple_of(jb * row_blk, row_blk)
            for q in range(row_blk // 8):
                acc = bias
                for k in range(KW):
                    acc = acc + full_ref[pl.ds(r0 + 8 * q + n_seq * k, 8), lanes] * wk[k]
                yc_ref[pl.ds(r0 + 8 * q, 8), lanes] = acc
            return carry

        lax.fori_loop(0, rows // row_blk, body, 0)


def _conv_tail(yc, x, lng_ref, lnb_ref, w2_ref):
    mu = jnp.mean(yc, axis=-1, keepdims=True)
    xc = yc - mu
    var = jnp.mean(xc * xc, axis=-1, keepdims=True)
    yn = xc * lax.rsqrt(var + EPS) * lng_ref[...] + lnb_ref[...]
    z = jax.nn.silu(yn).astype(BF16)
    return x + _dot(z, w2_ref[...])


def _conv_prompt_kernel(n_seq, v_ref, x_ref, wdw_ref, bdw_ref, lng_ref, lnb_ref, w2_ref,
                        o_ref, full_ref, yc_ref):
    hist = HIST * n_seq

    @pl.when(pl.program_id(0) == 0)
    def _():
        full_ref[0:hist, :] = jnp.zeros((hist, D), F32)

    full_ref[hist:hist + ROWS, :] = v_ref[...]
    _depthwise_conv(full_ref, yc_ref, wdw_ref, bdw_ref, n_seq, ROWS)
    full_ref[0:hist, :] = full_ref[ROWS:ROWS + hist, :]
    o_ref[...] = _conv_tail(yc_ref[...], x_ref[...], lng_ref, lnb_ref, w2_ref)


def _conv_sample_kernel(n_seq, full_ref, x_ref, wdw_ref, bdw_ref, lng_ref, lnb_ref, w2_ref,
                        o_ref, yc_ref):
    _depthwise_conv(full_ref, yc_ref, wdw_ref, bdw_ref, n_seq, ROWS)
    o_ref[...] = _conv_tail(yc_ref[...], x_ref[...], lng_ref, lnb_ref, w2_ref)


def _conv_prompt(v, x, n_seq, consts):
    n_t = v.shape[0] // ROWS
    tile = pl.BlockSpec((ROWS, D), lambda i: (i, 0))
    return pl.pallas_call(
        functools.partial(_conv_prompt_kernel, n_seq),
        grid=(n_t,),
        in_specs=[tile, tile] + [_const_spec(c.shape) for c in consts],
        out_specs=tile,
        out_shape=jax.ShapeDtypeStruct(v.shape, F32),
        scratch_shapes=[pltpu.VMEM((HIST * n_seq + ROWS, D), F32), pltpu.VMEM((ROWS, D), F32)],
        compiler_params=pltpu.CompilerParams(
            dimension_semantics=("arbitrary",), vmem_limit_bytes=VMEM_LIMIT),
        name="conv_prompt",
    )(v, x, *consts)


def _conv_sample(full, x, n_seq, consts):
    assert full.shape == (HIST * n_seq + ROWS, D) and x.shape == (ROWS, D)
    return pl.pallas_call(
        functools.partial(_conv_sample_kernel, n_seq),
        grid=(1,),
        in_specs=[_const_spec(full.shape), _const_spec(x.shape)]
        + [_const_spec(c.shape) for c in consts],
        out_specs=pl.BlockSpec((ROWS, D), lambda i: (0, 0)),
        out_shape=jax.ShapeDtypeStruct(x.shape, F32),
        scratch_shapes=[pltpu.VMEM((ROWS, D), F32)],
        compiler_params=pltpu.CompilerParams(
            dimension_semantics=("arbitrary",), vmem_limit_bytes=VMEM_LIMIT),
        name="conv_sample",
    )(full, x, *consts)


def _s5_prep_kernel(lr_ref, li_ref, ldt_ref, lrx_ref, lix_ref, bre_ref, bim_ref,
                    are_ref, aim_ref, wre_ref, wim_ref):
    dt = jnp.exp(ldt_ref[...])

    def disc(lr, li):
        mag = jnp.exp(lr * dt)
        ang = li * dt
        ab_re = mag * jnp.cos(ang)
        ab_im = mag * jnp.sin(ang)
        nr = ab_re - 1.0
        den = lr * lr + li * li
        cf_re = (nr * lr + ab_im * li) / den
        cf_im = (ab_im * lr - nr * li) / den
        return ab_re, ab_im, cf_re, cf_im

    ab_re, ab_im, _, _ = disc(lr_ref[...], li_ref[...])
    are_ref[...] = ab_re
    aim_ref[...] = ab_im
    _, _, cf_re, cf_im = disc(lrx_ref[...], lix_ref[...])
    b_re = bre_ref[...]
    b_im = bim_ref[...]
    wre_ref[...] = cf_re * b_re - cf_im * b_im
    wim_ref[...] = cf_re * b_im + cf_im * b_re


def _s5_prep(lam_re, lam_im, log_dt, b_re, b_im):
    lrx = jnp.repeat(lam_re, C, axis=1)
    lix = jnp.repeat(lam_im, C, axis=1)
    full = lambda s: pl.BlockSpec(s, lambda: (0,) * len(s))
    ins = (lam_re, lam_im, log_dt.reshape(G, 1), lrx, lix,
           b_re.reshape(G, P * C), b_im.reshape(G, P * C))
    outs = [jax.ShapeDtypeStruct((G, P), F32)] * 2 + [jax.ShapeDtypeStruct((G, P * C), F32)] * 2
    return pl.pallas_call(
        _s5_prep_kernel,
        in_specs=[full(a.shape) for a in ins],
        out_specs=[full(o.shape) for o in outs],
        out_shape=outs,
        name="s5_prep",
    )(*ins)


def _block_diag(w):
    j, g, a, b = w.shape
    eye = jnp.eye(g, dtype=w.dtype)
    return jnp.einsum("jgab,gh->jgahb", w, eye).reshape(j, g * a, g * b)


def _s5_kernel(carry, n_seq, n_t, x_ref, h0_ref, g_ref, wb_ref, wc_ref, are_ref, aim_ref,
               d_ref, wglu_ref, bglu_ref, o_ref, st_ref, bu_ref, y_ref):
    rows = n_t * n_seq
    if carry:
        @pl.when(pl.program_id(0) == 0)
        def _():
            st_ref[...] = h0_ref[...]
    else:
        st_ref[...] = h0_ref[...]

    x = x_ref[...].reshape(rows, D)
    h = _rms(x, g_ref[...])
    hb = h.astype(BF16)
    for j in range(N_COLGRP):
        bu_ref[...] = _dot(hb[:, j * COL_CH:(j + 1) * COL_CH], wb_ref[j])
        a_re = are_ref[j]
        a_im = aim_ref[j]

        def seq_block(sb, c):
            s0 = pl.multiple_of(sb * 8, 8)
            sr0 = st_ref[j, pl.ds(s0, 8), 0:COL_ST]
            si0 = st_ref[j, pl.ds(s0, 8), COL_ST:2 * COL_ST]

            def step(t, st):
                sr, si = st
                row = pl.multiple_of(t * n_seq + s0, 8)
                xr = bu_ref[pl.ds(row, 8), 0:COL_ST]
                xi = bu_ref[pl.ds(row, 8), COL_ST:2 * COL_ST]
                nr = a_re * sr - a_im * si + xr
                ni = a_re * si + a_im * sr + xi
                bu_ref[pl.ds(row, 8), 0:COL_ST] = nr
                bu_ref[pl.ds(row, 8), COL_ST:2 * COL_ST] = ni
                return nr, ni

            sr, si = lax.fori_loop(0, n_t, step, (sr0, si0))
            st_ref[j, pl.ds(s0, 8), 0:COL_ST] = sr
            st_ref[j, pl.ds(s0, 8), COL_ST:2 * COL_ST] = si
            return c

        lax.fori_loop(0, n_seq // 8, seq_block, 0)
        y_ref[:, j * COL_CH:(j + 1) * COL_CH] = _dot(bu_ref[...].astype(BF16), wc_ref[j])

    y = y_ref[...] + d_ref[...] * h
    yg = jax.nn.gelu(y).astype(BF16)
    z = _dot(yg, wglu_ref[...]) + bglu_ref[...]
    out = x + z[:, :D] * jax.nn.sigmoid(z[:, D:])
    o_ref[...] = out.reshape(n_t, n_seq, D)


def _s5_layer(x3, h0, consts, carry, n_t, n_seq, name):
    t_all, n_all, _ = x3.shape
    if carry:
        assert n_seq == n_all
        grid = (t_all // n_t,)
        x_spec = pl.BlockSpec((n_t, n_seq, D), lambda i: (i, 0, 0))
        st_spec = pl.BlockSpec((N_COLGRP, n_seq, 2 * COL_ST), lambda i: (0, 0, 0))
    else:
        assert n_t == t_all
        grid = (n_all // n_seq,)
        x_spec = pl.BlockSpec((n_t, n_seq, D), lambda i: (0, i, 0))
        st_spec = pl.BlockSpec((N_COLGRP, n_seq, 2 * COL_ST), lambda i: (0, i, 0))
    rows = n_t * n_seq
    return pl.pallas_call(
        functools.partial(_s5_kernel, carry, n_seq, n_t),
        grid=grid,
        in_specs=[x_spec, st_spec] + [_const_spec(c.shape) for c in consts],
        out_specs=[x_spec, st_spec],
        out_shape=[jax.ShapeDtypeStruct(x3.shape, F32), jax.ShapeDtypeStruct(h0.shape, F32)],
        scratch_shapes=[pltpu.VMEM((rows, 2 * COL_ST), F32), pltpu.VMEM((rows, D), F32)],
        compiler_params=pltpu.CompilerParams(
            dimension_semantics=("arbitrary",), vmem_limit_bytes=VMEM_LIMIT),
        name=name,
    )(x3, h0, *consts)


def _state_to_colgrp(s_re, s_im):
    n = s_re.shape[0]
    re = s_re.reshape(n, N_COLGRP, COL_ST)
    im = s_im.reshape(n, N_COLGRP, COL_ST)
    return jnp.concatenate([re, im], axis=-1).transpose(1, 0, 2)


def _state_from_colgrp(st):
    n = st.shape[1]
    st = st.transpose(1, 0, 2)
    return st[..., :COL_ST].reshape(n, G, P), st[..., COL_ST:].reshape(n, G, P)


def kernel(x_prompt, x_sample, cache_conv, state_ssm_re, state_ssm_im, norm_mix, norm_ffn, norm_final, conv_w_pw1, conv_b_pw1, conv_w_dw, conv_b_dw, conv_ln_g, conv_ln_b, conv_w_pw2, ssm_lam_re, ssm_lam_im, ssm_log_dt, ssm_b_re, ssm_b_im, ssm_c_re, ssm_c_im, ssm_d, ssm_w_glu, ssm_b_glu, ffn_w_gate, ffn_w_up, ffn_w_down):
    n_p, t_p, _ = x_prompt.shape
    n_s, t_s, _ = x_sample.shape
    row = lambda a: a.reshape(1, -1)

    xp = x_prompt.transpose(1, 0, 2).reshape(t_p * n_p, D)
    xs = x_sample.transpose(1, 0, 2).reshape(t_s * n_s, D)

    vp, vs = _rowwise_call(
        _pw1_body, xp, xs,
        (row(norm_mix[0]), conv_w_pw1[0].astype(BF16), row(conv_b_pw1[0])), name="pw1")
    conv_consts = (conv_w_dw[0], row(conv_b_dw[0]), row(conv_ln_g[0]), row(conv_ln_b[0]),
                   conv_w_pw2[0].astype(BF16))
    y1p = _conv_prompt(vp, xp, n_p, conv_consts)
    cache_tm = cache_conv[0].transpose(1, 0, 2).reshape(HIST * n_s, D)
    full_s = jnp.concatenate([cache_tm, vs], axis=0)
    y1s = _conv_sample(full_s, xs, n_s, conv_consts)
    conv_p = vp[-HIST * n_p:].reshape(HIST, n_p, D).transpose(1, 0, 2)[None]
    conv_s = full_s[-HIST * n_s:].reshape(HIST, n_s, D).transpose(1, 0, 2)[None]

    ffn0 = (row(norm_ffn[0]), ffn_w_gate[0].astype(BF16), ffn_w_up[0].astype(BF16),
            ffn_w_down[0].astype(BF16))
    y2p, y2s = _rowwise_call(_ffn_body, y1p, y1s, ffn0, name="ffn0")

    ab_re, ab_im, w_re, w_im = _s5_prep(ssm_lam_re[0], ssm_lam_im[0], ssm_log_dt[0],
                                        ssm_b_re[0], ssm_b_im[0])
    to_b = lambda w: _block_diag(
        w.reshape(N_COLGRP, GRP_PER_COL, P, C).transpose(0, 1, 3, 2))
    wb = jnp.concatenate([to_b(w_re), to_b(w_im)], axis=-1).astype(BF16)
    to_c = lambda w: _block_diag(
        w.reshape(N_COLGRP, GRP_PER_COL, C, P).transpose(0, 1, 3, 2))
    wc = jnp.concatenate([to_c(ssm_c_re[0]), to_c(-ssm_c_im[0])], axis=1).astype(BF16)
    bcast = lambda a: jnp.broadcast_to(a.reshape(N_COLGRP, 1, COL_ST), (N_COLGRP, 8, COL_ST))
    s5_consts = (row(norm_mix[1]), wb, wc, bcast(ab_re), bcast(ab_im), row(ssm_d[0]),
                 ssm_w_glu[0].astype(BF16), row(ssm_b_glu[0]))

    zero_state = jnp.zeros((n_p, G, P), F32)
    y3p, stp = _s5_layer(y2p.reshape(t_p, n_p, D), _state_to_colgrp(zero_state, zero_state),
                         s5_consts, True, SSM_ROWS // n_p, n_p, "s5_prompt")
    y3s, sts = _s5_layer(y2s.reshape(t_s, n_s, D),
                         _state_to_colgrp(state_ssm_re[0], state_ssm_im[0]),
                         s5_consts, False, t_s, SSM_ROWS // t_s, "s5_sample")
    ssm_p_re, ssm_p_im = _state_from_colgrp(stp)
    ssm_s_re, ssm_s_im = _state_from_colgrp(sts)

    ffn1 = (row(norm_ffn[1]), ffn_w_gate[1].astype(BF16), ffn_w_up[1].astype(BF16),
            ffn_w_down[1].astype(BF16), row(norm_final))
    outp, outs = _rowwise_call(_ffn_final_body, y3p.reshape(t_p * n_p, D),
                               y3s.reshape(t_s * n_s, D), ffn1, name="ffn1")

    y_prompt = outp.reshape(t_p, n_p, D).transpose(1, 0, 2)
    y_sample = outs.reshape(t_s, n_s, D).transpose(1, 0, 2)
    return (y_prompt, y_sample, conv_p, conv_s,
            ssm_p_re[None], ssm_p_im[None], ssm_s_re[None], ssm_s_im[None])
```

```python
import functools

import jax
import jax.numpy as jnp
from jax import lax
from jax.experimental import pallas as pl
from jax.experimental.pallas import tpu as pltpu

D = 1024
F = 2816
KW = 31
HIST = KW - 1
G = 64
P = 64
C = 16
EPS = 1e-6

SUBLANES = 8
LANES = 128
MXU_DIM = 256

ROWS = 1024
L0_ROWS = 512
SSM_ROWS = 512
N_COLGRP = 4
GRP_PER_COL = G // N_COLGRP
COL_CH = D // N_COLGRP
COL_ST = GRP_PER_COL * P
FFN_CHUNKS = ((0, 768), (768, 768), (1536, 768), (2304, 512))
FFN_CW = MXU_DIM
N_FFN_CHUNK = F // FFN_CW
VMEM_LIMIT = 56 * 1024 * 1024

BF16 = jnp.bfloat16
F32 = jnp.float32


def _const_spec(shape):
    nd = len(shape)
    return pl.BlockSpec(shape, lambda i: (0,) * nd, pipeline_mode=pl.Buffered(1))


def _rms(x, g):
    return x * lax.rsqrt(jnp.mean(x * x, axis=-1, keepdims=True) + EPS) * g


def _dot(a, b):
    return jnp.dot(a, b, preferred_element_type=F32)


def _params(limit=VMEM_LIMIT):
    return pltpu.CompilerParams(dimension_semantics=("arbitrary",), vmem_limit_bytes=limit)


def _pw1_kernel(n_p, xp_ref, xs_ref, g_ref, w_ref, b_ref, vp_ref, xtp_ref, vs_ref, xts_ref):
    i = pl.program_id(0)

    def run(x3_ref, v_ref, xt_ref):
        xt = jnp.swapaxes(x3_ref[...], 0, 1).reshape(ROWS, D)
        xt_ref[...] = xt
        h = _rms(xt, g_ref[...]).astype(BF16)
        a = _dot(h, w_ref[...]) + b_ref[...]
        v_ref[...] = a[:, :D] * jax.nn.sigmoid(a[:, D:])

    @pl.when(i < n_p)
    def _():
        run(xp_ref, vp_ref, xtp_ref)

    @pl.when(i == n_p)
    def _():
        run(xs_ref, vs_ref, xts_ref)


def _pw1(x_prompt, x_sample, consts):
    n_seq, t_p, _ = x_prompt.shape
    n_s, t_s, _ = x_sample.shape
    tb = ROWS // n_seq
    n_p = t_p // tb
    assert n_s * t_s == ROWS and t_p % tb == 0
    p_in = pl.BlockSpec((n_seq, tb, D), lambda i: (0, jnp.minimum(i, n_p - 1), 0))
    s_in = pl.BlockSpec((n_s, t_s, D), lambda i: (0, 0, 0))
    p_out = pl.BlockSpec((ROWS, D), lambda i: (jnp.minimum(i, n_p - 1), 0))
    s_out = pl.BlockSpec((ROWS, D), lambda i: (0, 0))
    p_shape = jax.ShapeDtypeStruct((t_p * n_seq, D), F32)
    s_shape = jax.ShapeDtypeStruct((ROWS, D), F32)
    return pl.pallas_call(
        functools.partial(_pw1_kernel, n_p),
        grid=(n_p + 1,),
        in_specs=[p_in, s_in] + [_const_spec(c.shape) for c in consts],
        out_specs=[p_out, p_out, s_out, s_out],
        out_shape=[p_shape, p_shape, s_shape, s_shape],
        compiler_params=_params(),
        name="pw1",
    )(x_prompt, x_sample, *consts)


def _ffn(x, g_ref, wg_ref, wu_ref, wd_ref):
    h = _rms(x, g_ref[...]).astype(BF16)
    acc = None
    for c0, cw in FFN_CHUNKS:
        gate = _dot(h, wg_ref[:, c0:c0 + cw])
        up = _dot(h, wu_ref[:, c0:c0 + cw])
        act = (jax.nn.silu(gate) * up).astype(BF16)
        part = _dot(act, wd_ref[c0:c0 + cw, :])
        acc = part if acc is None else acc + part
    return x + acc


def _ffn_final_kernel(n_p, n_seq, xp_ref, xs_ref, g_ref, wg_ref, wu_ref, wd_ref, gf_ref,
                      op_ref, os_ref):
    i = pl.program_id(0)

    def run(x_ref, o_ref):
        y = _rms(_ffn(x_ref[...], g_ref, wg_ref, wu_ref, wd_ref), gf_ref[...])
        n, t, _ = o_ref.shape
        o_ref[...] = jnp.swapaxes(y.reshape(t, n, D), 0, 1)

    @pl.when(i < n_p)
    def _():
        run(xp_ref, op_ref)

    @pl.when(i == n_p)
    def _():
        run(xs_ref, os_ref)


def _ffn_final(xp, xs, n_seq, n_s, consts):
    n_p = xp.shape[0] // ROWS
    tb = ROWS // n_seq
    t_s = ROWS // n_s
    p_in = pl.BlockSpec((ROWS, D), lambda i: (jnp.minimum(i, n_p - 1), 0))
    s_in = pl.BlockSpec((ROWS, D), lambda i: (0, 0))
    p_out = pl.BlockSpec((n_seq, tb, D), lambda i: (0, jnp.minimum(i, n_p - 1), 0))
    s_out = pl.BlockSpec((n_s, t_s, D), lambda i: (0, 0, 0))
    return pl.pallas_call(
        functools.partial(_ffn_final_kernel, n_p, n_seq),
        grid=(n_p + 1,),
        in_specs=[p_in, s_in] + [_const_spec(c.shape) for c in consts],
        out_specs=[p_out, s_out],
        out_shape=[jax.ShapeDtypeStruct((n_seq, n_p * tb, D), F32),
                   jax.ShapeDtypeStruct((n_s, t_s, D), F32)],
        compiler_params=_params(),
        name="ffn1",
    )(xp, xs, *consts)


def _conv_sample_kernel(n_seq, n_t, full_ref, wdw_ref, bdw_ref, yc_ref):
    for lb in range(D // LANES):
        lanes = slice(lb * LANES, (lb + 1) * LANES)
        wk = [jnp.broadcast_to(wdw_ref[k:k + 1, lanes], (SUBLANES, LANES)) for k in range(KW)]
        bias = jnp.broadcast_to(bdw_ref[0:1, lanes], (SUBLANES, LANES))

        def body(sb, carry):
            s0 = pl.multiple_of(sb * SUBLANES, SUBLANES)
            xs = [full_ref[pl.ds(s0 + j * n_seq, SUBLANES), lanes] for j in range(n_t + HIST)]
            for t in range(n_t):
                parts = [bias, None]
                for k in range(KW):
                    term = xs[t + k] * wk[k]
                    parts[k % 2] = term if parts[k % 2] is None else parts[k % 2] + term
                yc_ref[pl.ds(s0 + t * n_seq, SUBLANES), lanes] = parts[0] + parts[1]
            return carry

        lax.fori_loop(0, n_seq // SUBLANES, body, 0)


def _conv_sample(full, n_seq, n_t, wdw, bdw):
    rows = n_seq * n_t
    assert full.shape == ((HIST + n_t) * n_seq, D)
    return pl.pallas_call(
        functools.partial(_conv_sample_kernel, n_seq, n_t),
        grid=(1,),
        in_specs=[_const_spec(full.shape), _const_spec(wdw.shape), _const_spec(bdw.shape)],
        out_specs=pl.BlockSpec((rows, D), lambda i: (0, 0)),
        out_shape=jax.ShapeDtypeStruct((rows, D), F32),
        compiler_params=_params(),
        name="conv_sample",
    )(full, wdw, bdw)


def _l0_kernel(n_pt, v_ref, x_ref, ycs_ref, xs_ref, wdw_ref, bdw_ref, lng_ref, lnb_ref, w2_ref,
               gf_ref, wg_ref, wu_ref, wd_ref, op_ref, os_ref, full_ref, yc_ref, h_ref):
    s = pl.program_id(0)
    rows = L0_ROWS
    hist = HIST * SUBLANES
    steps_per_chunk = -(-(rows // SUBLANES) // N_FFN_CHUNK)

    def conv_slice(c):
        base = jnp.minimum(c * steps_per_chunk, rows // SUBLANES - steps_per_chunk)
        r0 = pl.multiple_of(base * SUBLANES, SUBLANES)
        for lb in range(D // LANES):
            lanes = slice(lb * LANES, (lb + 1) * LANES)
            wk = [jnp.broadcast_to(wdw_ref[k:k + 1, lanes], (SUBLANES, LANES)) for k in range(KW)]
            bias = jnp.broadcast_to(bdw_ref[0:1, lanes], (SUBLANES, LANES))
            for q in range(steps_per_chunk):
                parts = [bias, None]
                for k in range(KW):
                    term = full_ref[pl.ds(r0 + SUBLANES * (q + k), SUBLANES), lanes] * wk[k]
                    parts[k % 2] = term if parts[k % 2] is None else parts[k % 2] + term
                yc_ref[pl.ds(r0 + SUBLANES * q, SUBLANES), lanes] = parts[0] + parts[1]

    def ffn_chunk(c, o_ref):
        c0 = pl.multiple_of(c * FFN_CW, FFN_CW)
        h = h_ref[...]
        gate = _dot(h, wg_ref[:, pl.ds(c0, FFN_CW)])
        up = _dot(h, wu_ref[:, pl.ds(c0, FFN_CW)])
        act = (jax.nn.silu(gate) * up).astype(BF16)
        o_ref[...] += _dot(act, wd_ref[pl.ds(c0, FFN_CW), :])

    def prologue(yc, x, o_ref):
        mu = jnp.mean(yc, axis=-1, keepdims=True)
        xc = yc - mu
        var = jnp.mean(xc * xc, axis=-1, keepdims=True)
        yn = xc * lax.rsqrt(var + EPS) * lng_ref[...] + lnb_ref[...]
        y1 = x + _dot(jax.nn.silu(yn).astype(BF16), w2_ref[...])
        h_ref[...] = _rms(y1, gf_ref[...]).astype(BF16)
        o_ref[...] = y1

    def load_tile():
        full_ref[hist:hist + rows, :] = v_ref[...]

    def keep_history():
        full_ref[0:hist, :] = full_ref[rows:rows + hist, :]

    def loop(body):
        def step(c, carry):
            body(c)
            return carry
        lax.fori_loop(0, N_FFN_CHUNK, step, 0)

    @pl.when(s == 0)
    def _():
        full_ref[0:hist, :] = jnp.zeros((hist, D), F32)
        load_tile()
        loop(conv_slice)
        keep_history()

    @pl.when(jnp.logical_and(s > 0, s < n_pt))
    def _():
        prologue(yc_ref[...], x_ref[...], op_ref)
        load_tile()

        def both(c):
            conv_slice(c)
            ffn_chunk(c, op_ref)
        loop(both)
        keep_history()

    @pl.when(s == n_pt)
    def _():
        prologue(yc_ref[...], x_ref[...], op_ref)
        loop(lambda c: ffn_chunk(c, op_ref))

    @pl.when(s > n_pt)
    def _():
        prologue(ycs_ref[...], xs_ref[...], os_ref)
        loop(lambda c: ffn_chunk(c, os_ref))


def _layer0_tail(vp, xtp, ycs, xts, n_seq, consts):
    assert n_seq == SUBLANES, "prompt time step must be one 8-row tile"
    rows = L0_ROWS
    n_pt = vp.shape[0] // rows
    n_st = xts.shape[0] // rows
    tile = (rows, D)
    v_spec = pl.BlockSpec(tile, lambda s: (jnp.minimum(s, n_pt - 1), 0))
    p_spec = pl.BlockSpec(tile, lambda s: (jnp.clip(s - 1, 0, n_pt - 1), 0))
    s_idx = lambda s: (jnp.clip(s - n_pt - 1, 0, n_st - 1), 0)
    s_in = pl.BlockSpec(tile, s_idx, pipeline_mode=pl.Buffered(1))
    s_out = pl.BlockSpec(tile, s_idx)
    return pl.pallas_call(
        functools.partial(_l0_kernel, n_pt),
        grid=(n_pt + 1 + n_st,),
        in_specs=[v_spec, p_spec, s_in, s_in] + [_const_spec(c.shape) for c in consts],
        out_specs=[p_spec, s_out],
        out_shape=[jax.ShapeDtypeStruct(vp.shape, F32), jax.ShapeDtypeStruct(xts.shape, F32)],
        scratch_shapes=[pltpu.VMEM((HIST * n_seq + rows, D), F32), pltpu.VMEM((rows, D), F32),
                        pltpu.VMEM((rows, D), BF16)],
        compiler_params=_params(60 * 1024 * 1024),
        name="layer0_tail",
    )(vp, xtp, ycs, xts, *consts)


def _s5_prep_kernel(lr_ref, li_ref, ldt_ref, lrx_ref, lix_ref, bre_ref, bim_ref,
                    are_ref, aim_ref, wre_ref, wim_ref):
    dt = jnp.exp(ldt_ref[...])

    def disc(lr, li):
        mag = jnp.exp(lr * dt)
        ang = li * dt
        ab_re = mag * jnp.cos(ang)
        ab_im = mag * jnp.sin(ang)
        nr = ab_re - 1.0
        den = lr * lr + li * li
        cf_re = (nr * lr + ab_im * li) / den
        cf_im = (ab_im * lr - nr * li) / den
        return ab_re, ab_im, cf_re, cf_im

    ab_re, ab_im, _, _ = disc(lr_ref[...], li_ref[...])
    are_ref[...] = ab_re
    aim_ref[...] = ab_im
    _, _, cf_re, cf_im = disc(lrx_ref[...], lix_ref[...])
    b_re = bre_ref[...]
    b_im = bim_ref[...]
    wre_ref[...] = cf_re * b_re - cf_im * b_im
    wim_ref[...] = cf_re * b_im + cf_im * b_re


def _s5_prep(lam_re, lam_im, log_dt, b_re, b_im):
    lrx = jnp.repeat(lam_re, C, axis=1)
    lix = jnp.repeat(lam_im, C, axis=1)
    full = lambda s: pl.BlockSpec(s, lambda: (0,) * len(s))
    ins = (lam_re, lam_im, log_dt.reshape(G, 1), lrx, lix,
           b_re.reshape(G, P * C), b_im.reshape(G, P * C))
    outs = [jax.ShapeDtypeStruct((G, P), F32)] * 2 + [jax.ShapeDtypeStruct((G, P * C), F32)] * 2
    return pl.pallas_call(
        _s5_prep_kernel,
        in_specs=[full(a.shape) for a in ins],
        out_specs=[full(o.shape) for o in outs],
        out_shape=outs,
        name="s5_prep",
    )(*ins)


def _block_diag(w):
    j, g, a, b = w.shape
    eye = jnp.eye(g, dtype=w.dtype)
    return jnp.einsum("jgab,gh->jgahb", w, eye).reshape(j, g * a, g * b)


def _s5_kernel(carry, n_seq, n_t, x_ref, h0_ref, g_ref, wb_ref, wc_ref, are_ref, aim_ref,
               d_ref, wglu_ref, bglu_ref, o_ref, st_ref, bu_ref, y_ref):
    rows = n_t * n_seq
    if carry:
        @pl.when(pl.program_id(0) == 0)
        def _():
            st_ref[...] = h0_ref[...]
    else:
        st_ref[...] = h0_ref[...]

    x = x_ref[...].reshape(rows, D)
    h = _rms(x, g_ref[...])
    hb = h.astype(BF16)
    for j in range(N_COLGRP):
        bu_ref[...] = _dot(hb[:, j * COL_CH:(j + 1) * COL_CH], wb_ref[j])
        a_re = are_ref[j]
        a_im = aim_ref[j]

        def seq_block(sb, c):
            s0 = pl.multiple_of(sb * SUBLANES, SUBLANES)
            sr0 = st_ref[j, pl.ds(s0, SUBLANES), 0:COL_ST]
            si0 = st_ref[j, pl.ds(s0, SUBLANES), COL_ST:2 * COL_ST]

            def step(t, st):
                sr, si = st
                row = pl.multiple_of(t * n_seq + s0, SUBLANES)
                xr = bu_ref[pl.ds(row, SUBLANES), 0:COL_ST]
                xi = bu_ref[pl.ds(row, SUBLANES), COL_ST:2 * COL_ST]
                nr = a_re * sr - a_im * si + xr
                ni = a_re * si + a_im * sr + xi
                bu_ref[pl.ds(row, SUBLANES), 0:COL_ST] = nr
                bu_ref[pl.ds(row, SUBLANES), COL_ST:2 * COL_ST] = ni
                return nr, ni

            sr, si = lax.fori_loop(0, n_t, step, (sr0, si0))
            st_ref[j, pl.ds(s0, SUBLANES), 0:COL_ST] = sr
            st_ref[j, pl.ds(s0, SUBLANES), COL_ST:2 * COL_ST] = si
            return c

        lax.fori_loop(0, n_seq // SUBLANES, seq_block, 0)
        y_ref[:, j * COL_CH:(j + 1) * COL_CH] = _dot(bu_ref[...].astype(BF16), wc_ref[j])

    y = y_ref[...] + d_ref[...] * h
    yg = jax.nn.gelu(y).astype(BF16)
    z = _dot(yg, wglu_ref[...]) + bglu_ref[...]
    out = x + z[:, :D] * jax.nn.sigmoid(z[:, D:])
    o_ref[...] = out.reshape(n_t, n_seq, D)


def _s5_layer(x3, h0, consts, carry, n_t, n_seq, name):
    t_all, n_all, _ = x3.shape
    if carry:
        assert n_seq == n_all
        grid = (t_all // n_t,)
        x_spec = pl.BlockSpec((n_t, n_seq, D), lambda i: (i, 0, 0))
        st_spec = pl.BlockSpec((N_COLGRP, n_seq, 2 * COL_ST), lambda i: (0, 0, 0))
    else:
        assert n_t == t_all
        grid = (n_all // n_seq,)
        x_spec = pl.BlockSpec((n_t, n_seq, D), lambda i: (0, i, 0))
        st_spec = pl.BlockSpec((N_COLGRP, n_seq, 2 * COL_ST), lambda i: (0, i, 0))
    rows = n_t * n_seq
    return pl.pallas_call(
        functools.partial(_s5_kernel, carry, n_seq, n_t),
        grid=grid,
        in_specs=[x_spec, st_spec] + [_const_spec(c.shape) for c in consts],
        out_specs=[x_spec, st_spec],
        out_shape=[jax.ShapeDtypeStruct(x3.shape, F32), jax.ShapeDtypeStruct(h0.shape, F32)],
        scratch_shapes=[pltpu.VMEM((rows, 2 * COL_ST), F32), pltpu.VMEM((rows, D), F32)],
        compiler_params=_params(),
        name=name,
    )(x3, h0, *consts)


def _state_to_colgrp(s_re, s_im):
    n = s_re.shape[0]
    re = s_re.reshape(n, N_COLGRP, COL_ST)
    im = s_im.reshape(n, N_COLGRP, COL_ST)
    return jnp.concatenate([re, im], axis=-1).transpose(1, 0, 2)


def _state_from_colgrp(st):
    n = st.shape[1]
    st = st.transpose(1, 0, 2)
    return st[..., :COL_ST].reshape(n, G, P), st[..., COL_ST:].reshape(n, G, P)


def kernel(x_prompt, x_sample, cache_conv, state_ssm_re, state_ssm_im, norm_mix, norm_ffn, norm_final, conv_w_pw1, conv_b_pw1, conv_w_dw, conv_b_dw, conv_ln_g, conv_ln_b, conv_w_pw2, ssm_lam_re, ssm_lam_im, ssm_log_dt, ssm_b_re, ssm_b_im, ssm_c_re, ssm_c_im, ssm_d, ssm_w_glu, ssm_b_glu, ffn_w_gate, ffn_w_up, ffn_w_down):
    n_p, t_p, _ = x_prompt.shape
    n_s, t_s, _ = x_sample.shape
    row = lambda a: a.reshape(1, -1)

    vp, xtp, vs, xts = _pw1(
        x_prompt, x_sample,
        (row(norm_mix[0]), conv_w_pw1[0].astype(BF16), row(conv_b_pw1[0])))
    cache_tm = cache_conv[0].transpose(1, 0, 2).reshape(HIST * n_s, D)
    full_s = jnp.concatenate([cache_tm, vs], axis=0)
    ycs = _conv_sample(full_s, n_s, t_s, conv_w_dw[0], row(conv_b_dw[0]))
    conv_p = vp[-HIST * n_p:].reshape(HIST, n_p, D).transpose(1, 0, 2)[None]
    conv_s = full_s[-HIST * n_s:].reshape(HIST, n_s, D).transpose(1, 0, 2)[None]
    l0_consts = (conv_w_dw[0], row(conv_b_dw[0]), row(conv_ln_g[0]), row(conv_ln_b[0]),
                 conv_w_pw2[0].astype(BF16), row(norm_ffn[0]), ffn_w_gate[0].astype(BF16),
                 ffn_w_up[0].astype(BF16), ffn_w_down[0].astype(BF16))
    y2p, y2s = _layer0_tail(vp, xtp, ycs, xts, n_p, l0_consts)

    ab_re, ab_im, w_re, w_im = _s5_prep(ssm_lam_re[0], ssm_lam_im[0], ssm_log_dt[0],
                                        ssm_b_re[0], ssm_b_im[0])
    to_b = lambda w: _block_diag(
        w.reshape(N_COLGRP, GRP_PER_COL, P, C).transpose(0, 1, 3, 2))
    wb = jnp.concatenate([to_b(w_re), to_b(w_im)], axis=-1).astype(BF16)
    to_c = lambda w: _block_diag(
        w.reshape(N_COLGRP, GRP_PER_COL, C, P).transpose(0, 1, 3, 2))
    wc = jnp.concatenate([to_c(ssm_c_re[0]), to_c(-ssm_c_im[0])], axis=1).astype(BF16)
    bcast = lambda a: jnp.broadcast_to(a.reshape(N_COLGRP, 1, COL_ST),
                                       (N_COLGRP, SUBLANES, COL_ST))
    s5_consts = (row(norm_mix[1]), wb, wc, bcast(ab_re), bcast(ab_im), row(ssm_d[0]),
                 ssm_w_glu[0].astype(BF16), row(ssm_b_glu[0]))

    zero_state = jnp.zeros((n_p, G, P), F32)
    y3p, stp = _s5_layer(y2p.reshape(t_p, n_p, D), _state_to_colgrp(zero_state, zero_state),
                         s5_consts, True, SSM_ROWS // n_p, n_p, "s5_prompt")
    y3s, sts = _s5_layer(y2s.reshape(t_s, n_s, D),
                         _state_to_colgrp(state_ssm_re[0], state_ssm_im[0]),
                         s5_consts, False, t_s, SSM_ROWS // t_s, "s5_sample")
    ssm_p_re, ssm_p_im = _state_from_colgrp(stp)
    ssm_s_re, ssm_s_im = _state_from_colgrp(sts)

    ffn1 = (row(norm_ffn[1]), ffn_w_gate[1].astype(BF16), ffn_w_up[1].astype(BF16),
            ffn_w_down[1].astype(BF16), row(norm_final))
    y_prompt, y_sample = _ffn_final(y3p.reshape(t_p * n_p, D), y3s.reshape(t_s * n_s, D),
                                    n_p, n_s, ffn1)
    return (y_prompt, y_sample, conv_p, conv_s,
            ssm_p_re[None], ssm_p_im[None], ssm_s_re[None], ssm_s_im[None])
```

```python
import functools

import jax
import jax.numpy as jnp
from jax import lax
from jax.experimental import pallas as pl
from jax.experimental.pallas import tpu as pltpu

D = 1024
F = 2816
KW = 31
HIST = KW - 1
G = 64
P = 64
C = 16
EPS = 1e-6

SUBLANES = 8
LANES = 128

ROWS = 1024
SSM_ROWS = 512
N_COLGRP = 4
GRP_PER_COL = G // N_COLGRP
COL_CH = D // N_COLGRP
COL_ST = GRP_PER_COL * P
N_BU_SLOTS = 3
FFN_CHUNKS = ((0, 768), (768, 768), (1536, 768), (2304, 512))
VMEM_LIMIT = 56 * 1024 * 1024
FFN0_VMEM_LIMIT = 60 * 1024 * 1024

BF16 = jnp.bfloat16
F32 = jnp.float32


def _const_spec(shape):
    nd = len(shape)
    return pl.BlockSpec(shape, lambda i: (0,) * nd, pipeline_mode=pl.Buffered(1))


def _rms(x, g):
    return x * lax.rsqrt(jnp.mean(x * x, axis=-1, keepdims=True) + EPS) * g


def _dot(a, b):
    return jnp.dot(a, b, preferred_element_type=F32)


def _params(limit=VMEM_LIMIT):
    return pltpu.CompilerParams(dimension_semantics=("arbitrary",), vmem_limit_bytes=limit)


def _pw1_kernel(n_p, xp_ref, xs_ref, g_ref, w_ref, b_ref, vp_ref, xtp_ref, vs_ref, xts_ref):
    i = pl.program_id(0)

    def run(x3_ref, v_ref, xt_ref):
        xt = jnp.swapaxes(x3_ref[...], 0, 1).reshape(ROWS, D)
        xt_ref[...] = xt
        h = _rms(xt, g_ref[...]).astype(BF16)
        a = _dot(h, w_ref[...]) + b_ref[...]
        v_ref[...] = a[:, :D] * jax.nn.sigmoid(a[:, D:])

    @pl.when(i < n_p)
    def _():
        run(xp_ref, vp_ref, xtp_ref)

    @pl.when(i == n_p)
    def _():
        run(xs_ref, vs_ref, xts_ref)


def _pw1(x_prompt, x_sample, consts):
    n_seq, t_p, _ = x_prompt.shape
    n_s, t_s, _ = x_sample.shape
    tb = ROWS // n_seq
    n_p = t_p // tb
    assert n_s * t_s == ROWS and t_p % tb == 0
    p_in = pl.BlockSpec((n_seq, tb, D), lambda i: (0, jnp.minimum(i, n_p - 1), 0))
    s_in = pl.BlockSpec((n_s, t_s, D), lambda i: (0, 0, 0))
    p_out = pl.BlockSpec((ROWS, D), lambda i: (jnp.minimum(i, n_p - 1), 0))
    s_out = pl.BlockSpec((ROWS, D), lambda i: (0, 0))
    p_shape = jax.ShapeDtypeStruct((t_p * n_seq, D), F32)
    s_shape = jax.ShapeDtypeStruct((ROWS, D), F32)
    return pl.pallas_call(
        functools.partial(_pw1_kernel, n_p),
        grid=(n_p + 1,),
        in_specs=[p_in, s_in] + [_const_spec(c.shape) for c in consts],
        out_specs=[p_out, p_out, s_out, s_out],
        out_shape=[p_shape, p_shape, s_shape, s_shape],
        compiler_params=_params(),
        name="pw1",
    )(x_prompt, x_sample, *consts)


def _conv_blocks(full_ref, yc_ref, wdw_ref, bdw_ref, n_seq, n_t):
    t_blk = SUBLANES
    n_sb = n_seq // SUBLANES
    for lb in range(D // LANES):
        lanes = slice(lb * LANES, (lb + 1) * LANES)
        wk = [jnp.broadcast_to(wdw_ref[k:k + 1, lanes], (SUBLANES, LANES)) for k in range(KW)]
        bias = jnp.broadcast_to(bdw_ref[0:1, lanes], (SUBLANES, LANES))

        def body(u, carry):
            r0 = pl.multiple_of((u // n_sb) * (t_blk * n_seq) + (u % n_sb) * SUBLANES, SUBLANES)
            xs = [full_ref[pl.ds(r0 + j * n_seq, SUBLANES), lanes] for j in range(t_blk + HIST)]
            for t in range(t_blk):
                parts = [bias, None]
                for k in range(KW):
                    term = xs[t + k] * wk[k]
                    parts[k % 2] = term if parts[k % 2] is None else parts[k % 2] + term
                yc_ref[pl.ds(r0 + t * n_seq, SUBLANES), lanes] = parts[0] + parts[1]
            return carry

        lax.fori_loop(0, (n_t // t_blk) * n_sb, body, 0)


def _conv_prompt_kernel(n_seq, v_ref, wdw_ref, bdw_ref, yc_ref, full_ref):
    hist = HIST * n_seq

    @pl.when(pl.program_id(0) == 0)
    def _():
        full_ref[0:hist, :] = jnp.zeros((hist, D), F32)

    full_ref[hist:hist + ROWS, :] = v_ref[...]
    _conv_blocks(full_ref, yc_ref, wdw_ref, bdw_ref, n_seq, ROWS // n_seq)
    full_ref[0:hist, :] = full_ref[ROWS:ROWS + hist, :]


def _conv_sample_kernel(n_seq, n_t, full_ref, wdw_ref, bdw_ref, yc_ref):
    _conv_blocks(full_ref, yc_ref, wdw_ref, bdw_ref, n_seq, n_t)


def _conv_prompt(v, n_seq, wdw, bdw):
    tile = pl.BlockSpec((ROWS, D), lambda i: (i, 0))
    return pl.pallas_call(
        functools.partial(_conv_prompt_kernel, n_seq),
        grid=(v.shape[0] // ROWS,),
        in_specs=[tile, _const_spec(wdw.shape), _const_spec(bdw.shape)],
        out_specs=tile,
        out_shape=jax.ShapeDtypeStruct(v.shape, F32),
        scratch_shapes=[pltpu.VMEM((HIST * n_seq + ROWS, D), F32)],
        compiler_params=_params(),
        name="conv_prompt",
    )(v, wdw, bdw)


def _conv_sample(full, n_seq, n_t, wdw, bdw):
    rows = n_seq * n_t
    assert full.shape == ((HIST + n_t) * n_seq, D)
    return pl.pallas_call(
        functools.partial(_conv_sample_kernel, n_seq, n_t),
        grid=(1,),
        in_specs=[_const_spec(full.shape), _const_spec(wdw.shape), _const_spec(bdw.shape)],
        out_specs=pl.BlockSpec((rows, D), lambda i: (0, 0)),
        out_shape=jax.ShapeDtypeStruct((rows, D), F32),
        compiler_params=_params(),
        name="conv_sample",
    )(full, wdw, bdw)


def _ffn(x, g_ref, wg_ref, wu_ref, wd_ref):
    h = _rms(x, g_ref[...]).astype(BF16)
    acc = None
    for c0, cw in FFN_CHUNKS:
        gate = _dot(h, wg_ref[:, c0:c0 + cw])
        up = _dot(h, wu_ref[:, c0:c0 + cw])
        act = (jax.nn.silu(gate) * up).astype(BF16)
        part = _dot(act, wd_ref[c0:c0 + cw, :])
        acc = part if acc is None else acc + part
    return x + acc


def _ffn0_kernel(yc_ref, x_ref, lng_ref, lnb_ref, w2_ref, g_ref, wg_ref, wu_ref, wd_ref, o_ref):
    yc = yc_ref[...]
    mu = jnp.mean(yc, axis=-1, keepdims=True)
    xc = yc - mu
    var = jnp.mean(xc * xc, axis=-1, keepdims=True)
    yn = xc * lax.rsqrt(var + EPS) * lng_ref[...] + lnb_ref[...]
    y1 = x_ref[...] + _dot(jax.nn.silu(yn).astype(BF16), w2_ref[...])
    o_ref[...] = _ffn(y1, g_ref, wg_ref, wu_ref, wd_ref)


def _ffn0(yc, x, consts, name):
    tile = pl.BlockSpec((ROWS, D), lambda i: (i, 0))
    return pl.pallas_call(
        _ffn0_kernel,
        grid=(x.shape[0] // ROWS,),
        in_specs=[tile, tile] + [_const_spec(c.shape) for c in consts],
        out_specs=tile,
        out_shape=jax.ShapeDtypeStruct(x.shape, F32),
        compiler_params=_params(FFN0_VMEM_LIMIT),
        name=name,
    )(yc, x, *consts)


def _ffn_final_kernel(n_p, xp_ref, xs_ref, g_ref, wg_ref, wu_ref, wd_ref, gf_ref,
                      op_ref, os_ref):
    i = pl.program_id(0)

    def run(x_ref, o_ref):
        y = _rms(_ffn(x_ref[...], g_ref, wg_ref, wu_ref, wd_ref), gf_ref[...])
        n, t, _ = o_ref.shape
        o_ref[...] = jnp.swapaxes(y.reshape(t, n, D), 0, 1)

    @pl.when(i < n_p)
    def _():
        run(xp_ref, op_ref)

    @pl.when(i == n_p)
    def _():
        run(xs_ref, os_ref)


def _ffn_final(xp, xs, n_seq, n_s, consts):
    n_p = xp.shape[0] // ROWS
    tb = ROWS // n_seq
    t_s = ROWS // n_s
    p_in = pl.BlockSpec((ROWS, D), lambda i: (jnp.minimum(i, n_p - 1), 0))
    s_in = pl.BlockSpec((ROWS, D), lambda i: (0, 0))
    p_out = pl.BlockSpec((n_seq, tb, D), lambda i: (0, jnp.minimum(i, n_p - 1), 0))
    s_out = pl.BlockSpec((n_s, t_s, D), lambda i: (0, 0, 0))
    return pl.pallas_call(
        functools.partial(_ffn_final_kernel, n_p),
        grid=(n_p + 1,),
        in_specs=[p_in, s_in] + [_const_spec(c.shape) for c in consts],
        out_specs=[p_out, s_out],
        out_shape=[jax.ShapeDtypeStruct((n_seq, n_p * tb, D), F32),
                   jax.ShapeDtypeStruct((n_s, t_s, D), F32)],
        compiler_params=_params(),
        name="ffn1",
    )(xp, xs, *consts)


def _s5_prep_kernel(lr_ref, li_ref, ldt_ref, lrx_ref, lix_ref, bre_ref, bim_ref,
                    are_ref, aim_ref, wre_ref, wim_ref):
    dt = jnp.exp(ldt_ref[...])

    def disc(lr, li):
        mag = jnp.exp(lr * dt)
        ang = li * dt
        ab_re = mag * jnp.cos(ang)
        ab_im = mag * jnp.sin(ang)
        nr = ab_re - 1.0
        den = lr * lr + li * li
        cf_re = (nr * lr + ab_im * li) / den
        cf_im = (ab_im * lr - nr * li) / den
        return ab_re, ab_im, cf_re, cf_im

    ab_re, ab_im, _, _ = disc(lr_ref[...], li_ref[...])
    are_ref[...] = ab_re
    aim_ref[...] = ab_im
    _, _, cf_re, cf_im = disc(lrx_ref[...], lix_ref[...])
    b_re = bre_ref[...]
    b_im = bim_ref[...]
    wre_ref[...] = cf_re * b_re - cf_im * b_im
    wim_ref[...] = cf_re * b_im + cf_im * b_re


def _s5_prep(lam_re, lam_im, log_dt, b_re, b_im):
    lrx = jnp.repeat(lam_re, C, axis=1)
    lix = jnp.repeat(lam_im, C, axis=1)
    full = lambda s: pl.BlockSpec(s, lambda: (0,) * len(s))
    ins = (lam_re, lam_im, log_dt.reshape(G, 1), lrx, lix,
           b_re.reshape(G, P * C), b_im.reshape(G, P * C))
    outs = [jax.ShapeDtypeStruct((G, P), F32)] * 2 + [jax.ShapeDtypeStruct((G, P * C), F32)] * 2
    return pl.pallas_call(
        _s5_prep_kernel,
        in_specs=[full(a.shape) for a in ins],
        out_specs=[full(o.shape) for o in outs],
        out_shape=outs,
        name="s5_prep",
    )(*ins)


def _block_diag(w):
    _, rows, b = w.shape
    shape = (rows, GRP_PER_COL * b)
    row_grp = lax.broadcasted_iota(jnp.int32, shape, 0) // (rows // GRP_PER_COL)
    col_grp = lax.broadcasted_iota(jnp.int32, shape, 1) // b
    return jnp.where(row_grp == col_grp, jnp.tile(w, (1, 1, GRP_PER_COL)), 0)


def _s5_kernel(carry, n_seq, n_t, x_ref, h0_ref, g_ref, wb_ref, wc_ref, are_ref, aim_ref,
               d_ref, wglu_ref, bglu_ref, o_ref, st_ref, bu_ref, y_ref):
    rows = n_t * n_seq
    if carry:
        @pl.when(pl.program_id(0) == 0)
        def _():
            st_ref[...] = h0_ref[...]
    else:
        st_ref[...] = h0_ref[...]

    x = x_ref[...].reshape(rows, D)
    h = _rms(x, g_ref[...])
    hb = h.astype(BF16)

    def b_proj(j):
        bu_ref[j % N_BU_SLOTS] = _dot(hb[:, j * COL_CH:(j + 1) * COL_CH], wb_ref[j])

    def scan(j):
        buf = bu_ref.at[j % N_BU_SLOTS]
        a_re = are_ref[j]
        a_im = aim_ref[j]
        for sb in range(n_seq // SUBLANES):
            s0 = sb * SUBLANES
            sr = st_ref[j, s0:s0 + SUBLANES, 0:COL_ST]
            si = st_ref[j, s0:s0 + SUBLANES, COL_ST:2 * COL_ST]
            for t in range(n_t):
                r = t * n_seq + s0
                xr = buf[r:r + SUBLANES, 0:COL_ST]
                xi = buf[r:r + SUBLANES, COL_ST:2 * COL_ST]
                sr, si = a_re * sr - a_im * si + xr, a_re * si + a_im * sr + xi
                buf[r:r + SUBLANES, 0:COL_ST] = sr
                buf[r:r + SUBLANES, COL_ST:2 * COL_ST] = si
            st_ref[j, s0:s0 + SUBLANES, 0:COL_ST] = sr
            st_ref[j, s0:s0 + SUBLANES, COL_ST:2 * COL_ST] = si

    def c_proj(j):
        y_ref[:, j * COL_CH:(j + 1) * COL_CH] = _dot(
            bu_ref[j % N_BU_SLOTS].astype(BF16), wc_ref[j])

    b_proj(0)
    for j in range(N_COLGRP):
        if j + 1 < N_COLGRP:
            b_proj(j + 1)
        scan(j)
        c_proj(j)

    y = y_ref[...] + d_ref[...] * h
    yg = jax.nn.gelu(y).astype(BF16)
    z = _dot(yg, wglu_ref[...]) + bglu_ref[...]
    out = x + z[:, :D] * jax.nn.sigmoid(z[:, D:])
    o_ref[...] = out.reshape(n_t, n_seq, D)


def _s5_layer(x3, h0, consts, carry, n_t, n_seq, name):
    t_all, n_all, _ = x3.shape
    if carry:
        assert n_seq == n_all
        grid = (t_all // n_t,)
        x_spec = pl.BlockSpec((n_t, n_seq, D), lambda i: (i, 0, 0))
        st_spec = pl.BlockSpec((N_COLGRP, n_seq, 2 * COL_ST), lambda i: (0, 0, 0))
    else:
        assert n_t == t_all
        grid = (n_all // n_seq,)
        x_spec = pl.BlockSpec((n_t, n_seq, D), lambda i: (0, i, 0))
        st_spec = pl.BlockSpec((N_COLGRP, n_seq, 2 * COL_ST), lambda i: (0, i, 0))
    rows = n_t * n_seq
    return pl.pallas_call(
        functools.partial(_s5_kernel, carry, n_seq, n_t),
        grid=grid,
        in_specs=[x_spec, st_spec] + [_const_spec(c.shape) for c in consts],
        out_specs=[x_spec, st_spec],
        out_shape=[jax.ShapeDtypeStruct(x3.shape, F32), jax.ShapeDtypeStruct(h0.shape, F32)],
        scratch_shapes=[pltpu.VMEM((N_BU_SLOTS, rows, 2 * COL_ST), F32),
                        pltpu.VMEM((rows, D), F32)],
        compiler_params=_params(),
        name=name,
    )(x3, h0, *consts)


def _state_to_colgrp(s_re, s_im):
    n = s_re.shape[0]
    re = s_re.reshape(n, N_COLGRP, COL_ST)
    im = s_im.reshape(n, N_COLGRP, COL_ST)
    return jnp.concatenate([re, im], axis=-1).transpose(1, 0, 2)


def _state_from_colgrp(st):
    n = st.shape[1]
    st = st.transpose(1, 0, 2)
    return st[..., :COL_ST].reshape(n, G, P), st[..., COL_ST:].reshape(n, G, P)


def kernel(x_prompt, x_sample, cache_conv, state_ssm_re, state_ssm_im, norm_mix, norm_ffn, norm_final, conv_w_pw1, conv_b_pw1, conv_w_dw, conv_b_dw, conv_ln_g, conv_ln_b, conv_w_pw2, ssm_lam_re, ssm_lam_im, ssm_log_dt, ssm_b_re, ssm_b_im, ssm_c_re, ssm_c_im, ssm_d, ssm_w_glu, ssm_b_glu, ffn_w_gate, ffn_w_up, ffn_w_down):
    n_p, t_p, _ = x_prompt.shape
    n_s, t_s, _ = x_sample.shape
    row = lambda a: a.reshape(1, -1)

    vp, xtp, vs, xts = _pw1(
        x_prompt, x_sample,
        (row(norm_mix[0]), conv_w_pw1[0].astype(BF16), row(conv_b_pw1[0])))
    wdw, bdw = conv_w_dw[0], row(conv_b_dw[0])
    ycp = _conv_prompt(vp, n_p, wdw, bdw)
    cache_tm = cache_conv[0].transpose(1, 0, 2).reshape(HIST * n_s, D)
    full_s = jnp.concatenate([cache_tm, vs], axis=0)
    ycs = _conv_sample(full_s, n_s, t_s, wdw, bdw)
    conv_p = vp[-HIST * n_p:].reshape(HIST, n_p, D).transpose(1, 0, 2)[None]
    conv_s = full_s[-HIST * n_s:].reshape(HIST, n_s, D).transpose(1, 0, 2)[None]
    ffn0 = (row(conv_ln_g[0]), row(conv_ln_b[0]), conv_w_pw2[0].astype(BF16), row(norm_ffn[0]),
            ffn_w_gate[0].astype(BF16), ffn_w_up[0].astype(BF16), ffn_w_down[0].astype(BF16))
    y2p = _ffn0(ycp, xtp, ffn0, "ffn0_prompt")
    y2s = _ffn0(ycs, xts, ffn0, "ffn0_sample")

    ab_re, ab_im, w_re, w_im = _s5_prep(ssm_lam_re[0], ssm_lam_im[0], ssm_log_dt[0],
                                        ssm_b_re[0], ssm_b_im[0])
    to_b = lambda w: _block_diag(w.reshape(N_COLGRP, GRP_PER_COL, P, C).transpose(0, 1, 3, 2)
                                 .reshape(N_COLGRP, COL_CH, P))
    wb = jnp.concatenate([to_b(w_re), to_b(w_im)], axis=-1).astype(BF16)
    to_c = lambda w: _block_diag(w.reshape(N_COLGRP, GRP_PER_COL, C, P).transpose(0, 1, 3, 2)
                                 .reshape(N_COLGRP, COL_ST, C))
    wc = jnp.concatenate([to_c(ssm_c_re[0]), to_c(-ssm_c_im[0])], axis=1).astype(BF16)
    bcast = lambda a: jnp.broadcast_to(a.reshape(N_COLGRP, 1, COL_ST),
                                       (N_COLGRP, SUBLANES, COL_ST))
    s5_consts = (row(norm_mix[1]), wb, wc, bcast(ab_re), bcast(ab_im), row(ssm_d[0]),
                 ssm_w_glu[0].astype(BF16), row(ssm_b_glu[0]))

    zero_state = jnp.zeros((n_p, G, P), F32)
    y3p, stp = _s5_layer(y2p.reshape(t_p, n_p, D), _state_to_colgrp(zero_state, zero_state),
                         s5_consts, True, SSM_ROWS // n_p, n_p, "s5_prompt")
    y3s, sts = _s5_layer(y2s.reshape(t_s, n_s, D),
                         _state_to_colgrp(state_ssm_re[0], state_ssm_im[0]),
                         s5_consts, False, t_s, SSM_ROWS // t_s, "s5_sample")
    ssm_p_re, ssm_p_im = _state_from_colgrp(stp)
    ssm_s_re, ssm_s_im = _state_from_colgrp(sts)

    ffn1 = (row(norm_ffn[1]), ffn_w_gate[1].astype(BF16), ffn_w_up[1].astype(BF16),
            ffn_w_down[1].astype(BF16), row(norm_final))
    y_prompt, y_sample = _ffn_final(y3p.reshape(t_p * n_p, D), y3s.reshape(t_s * n_s, D),
                                    n_p, n_s, ffn1)
    return (y_prompt, y_sample, conv_p, conv_s,
            ssm_p_re[None], ssm_p_im[None], ssm_s_re[None], ssm_s_im[None])
```

```python
import functools

import jax
import jax.numpy as jnp
from jax import lax
from jax.experimental import pallas as pl
from jax.experimental.pallas import tpu as pltpu

D = 1024
F = 2816
KW = 31
HIST = KW - 1
G = 64
P = 64
C = 16
EPS = 1e-6

SUBLANES = 8
LANES = 128

ROWS = 1024
SSM_ROWS = 512
N_COLGRP = 4
GRP_PER_COL = G // N_COLGRP
COL_CH = D // N_COLGRP
COL_ST = GRP_PER_COL * P
N_BU_SLOTS = 3
FFN_CHUNKS = ((0, 768), (768, 768), (1536, 768), (2304, 512))
FFN0_ROWS = 512
VMEM_LIMIT = 56 * 1024 * 1024

BF16 = jnp.bfloat16
F32 = jnp.float32


def _const_spec(shape):
    nd = len(shape)
    return pl.BlockSpec(shape, lambda i: (0,) * nd, pipeline_mode=pl.Buffered(1))


def _rms(x, g):
    return x * lax.rsqrt(jnp.mean(x * x, axis=-1, keepdims=True) + EPS) * g


def _dot(a, b):
    return jnp.dot(a, b, preferred_element_type=F32)


def _params(limit=VMEM_LIMIT):
    return pltpu.CompilerParams(dimension_semantics=("arbitrary",), vmem_limit_bytes=limit)


def _pw1_kernel(n_p, xp_ref, xs_ref, g_ref, w_ref, b_ref, vp_ref, xtp_ref, vs_ref, xts_ref):
    i = pl.program_id(0)

    def run(x3_ref, v_ref, xt_ref):
        xt = jnp.swapaxes(x3_ref[...], 0, 1).reshape(ROWS, D)
        xt_ref[...] = xt
        h = _rms(xt, g_ref[...]).astype(BF16)
        a = _dot(h, w_ref[...]) + b_ref[...]
        v_ref[...] = a[:, :D] * jax.nn.sigmoid(a[:, D:])

    @pl.when(i < n_p)
    def _():
        run(xp_ref, vp_ref, xtp_ref)

    @pl.when(i == n_p)
    def _():
        run(xs_ref, vs_ref, xts_ref)


def _pw1(x_prompt, x_sample, consts):
    n_seq, t_p, _ = x_prompt.shape
    n_s, t_s, _ = x_sample.shape
    tb = ROWS // n_seq
    n_p = t_p // tb
    assert n_s * t_s == ROWS and t_p % tb == 0
    p_in = pl.BlockSpec((n_seq, tb, D), lambda i: (0, jnp.minimum(i, n_p - 1), 0))
    s_in = pl.BlockSpec((n_s, t_s, D), lambda i: (0, 0, 0))
    p_out = pl.BlockSpec((ROWS, D), lambda i: (jnp.minimum(i, n_p - 1), 0))
    s_out = pl.BlockSpec((ROWS, D), lambda i: (0, 0))
    p_shape = jax.ShapeDtypeStruct((t_p * n_seq, D), F32)
    s_shape = jax.ShapeDtypeStruct((ROWS, D), F32)
    return pl.pallas_call(
        functools.partial(_pw1_kernel, n_p),
        grid=(n_p + 1,),
        in_specs=[p_in, s_in] + [_const_spec(c.shape) for c in consts],
        out_specs=[p_out, p_out, s_out, s_out],
        out_shape=[p_shape, p_shape, s_shape, s_shape],
        compiler_params=_params(),
        name="pw1",
    )(x_prompt, x_sample, *consts)


def _conv_blocks(full_ref, yc_ref, wdw_ref, bdw_ref, n_seq, n_t):
    t_blk = SUBLANES
    n_sb = n_seq // SUBLANES
    for lb in range(D // LANES):
        lanes = slice(lb * LANES, (lb + 1) * LANES)
        wk = [jnp.broadcast_to(wdw_ref[k:k + 1, lanes], (SUBLANES, LANES)) for k in range(KW)]
        bias = jnp.broadcast_to(bdw_ref[0:1, lanes], (SUBLANES, LANES))

        def body(u, carry):
            r0 = pl.multiple_of((u // n_sb) * (t_blk * n_seq) + (u % n_sb) * SUBLANES, SUBLANES)
            xs = [full_ref[pl.ds(r0 + j * n_seq, SUBLANES), lanes] for j in range(t_blk + HIST)]
            for t in range(t_blk):
                parts = [bias, None]
                for k in range(KW):
                    term = xs[t + k] * wk[k]
                    parts[k % 2] = term if parts[k % 2] is None else parts[k % 2] + term
                yc_ref[pl.ds(r0 + t * n_seq, SUBLANES), lanes] = parts[0] + parts[1]
            return carry

        lax.fori_loop(0, (n_t // t_blk) * n_sb, body, 0)


def _conv_prompt_kernel(n_seq, v_ref, wdw_ref, bdw_ref, yc_ref, full_ref):
    hist = HIST * n_seq

    @pl.when(pl.program_id(0) == 0)
    def _():
        full_ref[0:hist, :] = jnp.zeros((hist, D), F32)

    full_ref[hist:hist + ROWS, :] = v_ref[...]
    _conv_blocks(full_ref, yc_ref, wdw_ref, bdw_ref, n_seq, ROWS // n_seq)
    full_ref[0:hist, :] = full_ref[ROWS:ROWS + hist, :]


def _conv_sample_kernel(n_seq, n_t, cache_ref, vs_ref, wdw_ref, bdw_ref, yc_ref, vn_ref,
                        full_ref):
    hist = HIST * n_seq
    for k in range(HIST):
        full_ref[k * n_seq:(k + 1) * n_seq, :] = cache_ref[:, k, :]
    vs = vs_ref[...]
    full_ref[hist:hist + n_t * n_seq, :] = vs
    vn_ref[...] = jnp.swapaxes(vs.reshape(n_t, n_seq, D), 0, 1)
    _conv_blocks(full_ref, yc_ref, wdw_ref, bdw_ref, n_seq, n_t)


def _conv_prompt(v, n_seq, wdw, bdw):
    tile = pl.BlockSpec((ROWS, D), lambda i: (i, 0))
    return pl.pallas_call(
        functools.partial(_conv_prompt_kernel, n_seq),
        grid=(v.shape[0] // ROWS,),
        in_specs=[tile, _const_spec(wdw.shape), _const_spec(bdw.shape)],
        out_specs=tile,
        out_shape=jax.ShapeDtypeStruct(v.shape, F32),
        scratch_shapes=[pltpu.VMEM((HIST * n_seq + ROWS, D), F32)],
        compiler_params=_params(),
        name="conv_prompt",
    )(v, wdw, bdw)


def _conv_sample(cache, vs, wdw, bdw):
    n_seq = cache.shape[0]
    rows = vs.shape[0]
    n_t = rows // n_seq
    assert cache.shape == (n_seq, HIST, D)
    return pl.pallas_call(
        functools.partial(_conv_sample_kernel, n_seq, n_t),
        grid=(1,),
        in_specs=[_const_spec(a.shape) for a in (cache, vs, wdw, bdw)],
        out_specs=[pl.BlockSpec((rows, D), lambda i: (0, 0)),
                   pl.BlockSpec((n_seq, n_t, D), lambda i: (0, 0, 0))],
        out_shape=[jax.ShapeDtypeStruct((rows, D), F32),
                   jax.ShapeDtypeStruct((n_seq, n_t, D), F32)],
        scratch_shapes=[pltpu.VMEM(((HIST + n_t) * n_seq, D), F32)],
        compiler_params=_params(),
        name="conv_sample",
    )(cache, vs, wdw, bdw)


def _ffn(x, g_ref, wg_ref, wu_ref, wd_ref):
    h = _rms(x, g_ref[...]).astype(BF16)
    acc = None
    for c0, cw in FFN_CHUNKS:
        gate = _dot(h, wg_ref[:, c0:c0 + cw])
        up = _dot(h, wu_ref[:, c0:c0 + cw])
        act = (jax.nn.silu(gate) * up).astype(BF16)
        part = _dot(act, wd_ref[c0:c0 + cw, :])
        acc = part if acc is None else acc + part
    return x + acc


def _ffn0_kernel(n_p, ycp_ref, xp_ref, ycs_ref, xs_ref, lng_ref, lnb_ref, w2_ref, g_ref,
                 wg_ref, wu_ref, wd_ref, op_ref, os_ref):
    i = pl.program_id(0)

    def run(yc_ref, x_ref, o_ref):
        yc = yc_ref[...]
        mu = jnp.mean(yc, axis=-1, keepdims=True)
        xc = yc - mu
        var = jnp.mean(xc * xc, axis=-1, keepdims=True)
        yn = xc * lax.rsqrt(var + EPS) * lng_ref[...] + lnb_ref[...]
        y1 = x_ref[...] + _dot(jax.nn.silu(yn).astype(BF16), w2_ref[...])
        o_ref[...] = _ffn(y1, g_ref, wg_ref, wu_ref, wd_ref)

    @pl.when(i < n_p)
    def _():
        run(ycp_ref, xp_ref, op_ref)

    @pl.when(i >= n_p)
    def _():
        run(ycs_ref, xs_ref, os_ref)


def _ffn0(ycp, xp, ycs, xs, consts):
    n_p = xp.shape[0] // FFN0_ROWS
    n_s = xs.shape[0] // FFN0_ROWS
    tile = (FFN0_ROWS, D)
    p_spec = pl.BlockSpec(tile, lambda i: (jnp.minimum(i, n_p - 1), 0))
    s_spec = pl.BlockSpec(tile, lambda i: (jnp.maximum(i - n_p, 0), 0))
    return pl.pallas_call(
        functools.partial(_ffn0_kernel, n_p),
        grid=(n_p + n_s,),
        in_specs=[p_spec, p_spec, s_spec, s_spec] + [_const_spec(c.shape) for c in consts],
        out_specs=[p_spec, s_spec],
        out_shape=[jax.ShapeDtypeStruct(xp.shape, F32), jax.ShapeDtypeStruct(xs.shape, F32)],
        compiler_params=_params(),
        name="ffn0",
    )(ycp, xp, ycs, xs, *consts)


def _ffn_final_kernel(n_p, xp_ref, xs_ref, g_ref, wg_ref, wu_ref, wd_ref, gf_ref,
                      op_ref, os_ref):
    i = pl.program_id(0)

    def run(x_ref, o_ref):
        y = _rms(_ffn(x_ref[...], g_ref, wg_ref, wu_ref, wd_ref), gf_ref[...])
        n, t, _ = o_ref.shape
        o_ref[...] = jnp.swapaxes(y.reshape(t, n, D), 0, 1)

    @pl.when(i < n_p)
    def _():
        run(xp_ref, op_ref)

    @pl.when(i == n_p)
    def _():
        run(xs_ref, os_ref)


def _ffn_final(xp, xs, n_seq, n_s, consts):
    n_p = xp.shape[0] // ROWS
    tb = ROWS // n_seq
    t_s = ROWS // n_s
    p_in = pl.BlockSpec((ROWS, D), lambda i: (jnp.minimum(i, n_p - 1), 0))
    s_in = pl.BlockSpec((ROWS, D), lambda i: (0, 0))
    p_out = pl.BlockSpec((n_seq, tb, D), lambda i: (0, jnp.minimum(i, n_p - 1), 0))
    s_out = pl.BlockSpec((n_s, t_s, D), lambda i: (0, 0, 0))
    return pl.pallas_call(
        functools.partial(_ffn_final_kernel, n_p),
        grid=(n_p + 1,),
        in_specs=[p_in, s_in] + [_const_spec(c.shape) for c in consts],
        out_specs=[p_out, s_out],
        out_shape=[jax.ShapeDtypeStruct((n_seq, n_p * tb, D), F32),
                   jax.ShapeDtypeStruct((n_s, t_s, D), F32)],
        compiler_params=_params(),
        name="ffn1",
    )(xp, xs, *consts)


def _s5_prep_kernel(lr_ref, li_ref, ldt_ref, lrx_ref, lix_ref, bre_ref, bim_ref,
                    are_ref, aim_ref, wre_ref, wim_ref):
    dt = jnp.exp(ldt_ref[...])

    def disc(lr, li):
        mag = jnp.exp(lr * dt)
        ang = li * dt
        ab_re = mag * jnp.cos(ang)
        ab_im = mag * jnp.sin(ang)
        nr = ab_re - 1.0
        den = lr * lr + li * li
        cf_re = (nr * lr + ab_im * li) / den
        cf_im = (ab_im * lr - nr * li) / den
        return ab_re, ab_im, cf_re, cf_im

    ab_re, ab_im, _, _ = disc(lr_ref[...], li_ref[...])
    are_ref[...] = ab_re
    aim_ref[...] = ab_im
    _, _, cf_re, cf_im = disc(lrx_ref[...], lix_ref[...])
    b_re = bre_ref[...]
    b_im = bim_ref[...]
    wre_ref[...] = cf_re * b_re - cf_im * b_im
    wim_ref[...] = cf_re * b_im + cf_im * b_re


def _s5_prep(lam_re, lam_im, log_dt, b_re, b_im):
    lrx = jnp.repeat(lam_re, C, axis=1)
    lix = jnp.repeat(lam_im, C, axis=1)
    full = lambda s: pl.BlockSpec(s, lambda: (0,) * len(s))
    ins = (lam_re, lam_im, log_dt.reshape(G, 1), lrx, lix,
           b_re.reshape(G, P * C), b_im.reshape(G, P * C))
    outs = [jax.ShapeDtypeStruct((G, P), F32)] * 2 + [jax.ShapeDtypeStruct((G, P * C), F32)] * 2
    return pl.pallas_call(
        _s5_prep_kernel,
        in_specs=[full(a.shape) for a in ins],
        out_specs=[full(o.shape) for o in outs],
        out_shape=outs,
        name="s5_prep",
    )(*ins)


def _block_diag(w):
    _, rows, b = w.shape
    shape = (rows, GRP_PER_COL * b)
    row_grp = lax.broadcasted_iota(jnp.int32, shape, 0) // (rows // GRP_PER_COL)
    col_grp = lax.broadcasted_iota(jnp.int32, shape, 1) // b
    return jnp.where(row_grp == col_grp, jnp.tile(w, (1, 1, GRP_PER_COL)), 0)


def _s5_kernel(carry, n_seq, n_t, x_ref, h0re_ref, h0im_ref, g_ref, wb_ref, wc_ref, are_ref,
               aim_ref, d_ref, wglu_ref, bglu_ref, o_ref, sre_ref, sim_ref, bu_ref, y_ref):
    rows = n_t * n_seq

    def init_state():
        sre_ref[...] = h0re_ref[...]
        sim_ref[...] = h0im_ref[...]

    if carry:
        pl.when(pl.program_id(0) == 0)(init_state)
    else:
        init_state()

    x = x_ref[...].reshape(rows, D)
    h = _rms(x, g_ref[...])
    hb = h.astype(BF16)

    def b_proj(j):
        bu_ref[j % N_BU_SLOTS] = _dot(hb[:, j * COL_CH:(j + 1) * COL_CH], wb_ref[j])

    def scan(j):
        buf = bu_ref.at[j % N_BU_SLOTS]
        st_lanes = slice(j * COL_ST, (j + 1) * COL_ST)
        a_re = jnp.broadcast_to(are_ref[:, st_lanes], (SUBLANES, COL_ST))
        a_im = jnp.broadcast_to(aim_ref[:, st_lanes], (SUBLANES, COL_ST))
        for sb in range(n_seq // SUBLANES):
            s0 = sb * SUBLANES
            sr = sre_ref[s0:s0 + SUBLANES, st_lanes]
            si = sim_ref[s0:s0 + SUBLANES, st_lanes]
            for t in range(n_t):
                r = t * n_seq + s0
                xr = buf[r:r + SUBLANES, 0:COL_ST]
                xi = buf[r:r + SUBLANES, COL_ST:2 * COL_ST]
                sr, si = a_re * sr - a_im * si + xr, a_re * si + a_im * sr + xi
                buf[r:r + SUBLANES, 0:COL_ST] = sr
                buf[r:r + SUBLANES, COL_ST:2 * COL_ST] = si
            sre_ref[s0:s0 + SUBLANES, st_lanes] = sr
            sim_ref[s0:s0 + SUBLANES, st_lanes] = si

    def c_proj(j):
        y_ref[:, j * COL_CH:(j + 1) * COL_CH] = _dot(
            bu_ref[j % N_BU_SLOTS].astype(BF16), wc_ref[j])

    b_proj(0)
    for j in range(N_COLGRP):
        if j + 1 < N_COLGRP:
            b_proj(j + 1)
        scan(j)
        c_proj(j)

    y = y_ref[...] + d_ref[...] * h
    yg = jax.nn.gelu(y).astype(BF16)
    z = _dot(yg, wglu_ref[...]) + bglu_ref[...]
    out = x + z[:, :D] * jax.nn.sigmoid(z[:, D:])
    o_ref[...] = out.reshape(n_t, n_seq, D)


def _s5_layer(x3, h0_re, h0_im, consts, carry, n_t, n_seq, name):
    t_all, n_all, _ = x3.shape
    if carry:
        assert n_seq == n_all
        grid = (t_all // n_t,)
        x_spec = pl.BlockSpec((n_t, n_seq, D), lambda i: (i, 0, 0))
        st_spec = pl.BlockSpec((n_seq, G * P), lambda i: (0, 0))
    else:
        assert n_t == t_all
        grid = (n_all // n_seq,)
        x_spec = pl.BlockSpec((n_t, n_seq, D), lambda i: (0, i, 0))
        st_spec = pl.BlockSpec((n_seq, G * P), lambda i: (i, 0))
    rows = n_t * n_seq
    st_shape = jax.ShapeDtypeStruct(h0_re.shape, F32)
    return pl.pallas_call(
        functools.partial(_s5_kernel, carry, n_seq, n_t),
        grid=grid,
        in_specs=[x_spec, st_spec, st_spec] + [_const_spec(c.shape) for c in consts],
        out_specs=[x_spec, st_spec, st_spec],
        out_shape=[jax.ShapeDtypeStruct(x3.shape, F32), st_shape, st_shape],
        scratch_shapes=[pltpu.VMEM((N_BU_SLOTS, rows, 2 * COL_ST), F32),
                        pltpu.VMEM((rows, D), F32)],
        compiler_params=_params(),
        name=name,
    )(x3, h0_re, h0_im, *consts)


def kernel(x_prompt, x_sample, cache_conv, state_ssm_re, state_ssm_im, norm_mix, norm_ffn, norm_final, conv_w_pw1, conv_b_pw1, conv_w_dw, conv_b_dw, conv_ln_g, conv_ln_b, conv_w_pw2, ssm_lam_re, ssm_lam_im, ssm_log_dt, ssm_b_re, ssm_b_im, ssm_c_re, ssm_c_im, ssm_d, ssm_w_glu, ssm_b_glu, ffn_w_gate, ffn_w_up, ffn_w_down):
    n_p, t_p, _ = x_prompt.shape
    n_s, t_s, _ = x_sample.shape
    row = lambda a: a.reshape(1, -1)

    vp, xtp, vs, xts = _pw1(
        x_prompt, x_sample,
        (row(norm_mix[0]), conv_w_pw1[0].astype(BF16), row(conv_b_pw1[0])))
    wdw, bdw = conv_w_dw[0], row(conv_b_dw[0])
    ycp = _conv_prompt(vp, n_p, wdw, bdw)
    ycs, vs_nt = _conv_sample(cache_conv[0], vs, wdw, bdw)
    conv_p = vp[-HIST * n_p:].reshape(HIST, n_p, D).transpose(1, 0, 2)[None]
    conv_s = jnp.concatenate([cache_conv[0][:, t_s:], vs_nt], axis=1)[None]
    ffn0 = (row(conv_ln_g[0]), row(conv_ln_b[0]), conv_w_pw2[0].astype(BF16), row(norm_ffn[0]),
            ffn_w_gate[0].astype(BF16), ffn_w_up[0].astype(BF16), ffn_w_down[0].astype(BF16))
    y2p, y2s = _ffn0(ycp, xtp, ycs, xts, ffn0)

    ab_re, ab_im, w_re, w_im = _s5_prep(ssm_lam_re[0], ssm_lam_im[0], ssm_log_dt[0],
                                        ssm_b_re[0], ssm_b_im[0])
    to_b = lambda w: _block_diag(w.reshape(N_COLGRP, GRP_PER_COL, P, C).transpose(0, 1, 3, 2)
                                 .reshape(N_COLGRP, COL_CH, P))
    wb = jnp.concatenate([to_b(w_re), to_b(w_im)], axis=-1).astype(BF16)
    to_c = lambda w: _block_diag(w.reshape(N_COLGRP, GRP_PER_COL, C, P).transpose(0, 1, 3, 2)
                                 .reshape(N_COLGRP, COL_ST, C))
    wc = jnp.concatenate([to_c(ssm_c_re[0]), to_c(-ssm_c_im[0])], axis=1).astype(BF16)
    s5_consts = (row(norm_mix[1]), wb, wc, row(ab_re), row(ab_im), row(ssm_d[0]),
                 ssm_w_glu[0].astype(BF16), row(ssm_b_glu[0]))

    zero_state = jnp.zeros((n_p, G * P), F32)
    y3p, ssm_p_re, ssm_p_im = _s5_layer(
        y2p.reshape(t_p, n_p, D), zero_state, zero_state,
        s5_consts, True, SSM_ROWS // n_p, n_p, "s5_prompt")
    y3s, ssm_s_re, ssm_s_im = _s5_layer(
        y2s.reshape(t_s, n_s, D), state_ssm_re[0].reshape(n_s, G * P),
        state_ssm_im[0].reshape(n_s, G * P),
        s5_consts, False, t_s, SSM_ROWS // t_s, "s5_sample")
    to_gp = lambda st: st.reshape(1, -1, G, P)

    ffn1 = (row(norm_ffn[1]), ffn_w_gate[1].astype(BF16), ffn_w_up[1].astype(BF16),
            ffn_w_down[1].astype(BF16), row(norm_final))
    y_prompt, y_sample = _ffn_final(y3p.reshape(t_p * n_p, D), y3s.reshape(t_s * n_s, D),
                                    n_p, n_s, ffn1)
    return (y_prompt, y_sample, conv_p, conv_s,
            to_gp(ssm_p_re), to_gp(ssm_p_im), to_gp(ssm_s_re), to_gp(ssm_s_im))
```

```python
import functools

import jax
import jax.numpy as jnp
from jax import lax
from jax.experimental import pallas as pl
from jax.experimental.pallas import tpu as pltpu

D = 1024
F = 2816
KW = 31
HIST = KW - 1
G = 64
P = 64
C = 16
EPS = 1e-6

SUBLANES = 8
LANES = 128

ROWS = 1024
SSM_ROWS = 512
N_COLGRP = 4
GRP_PER_COL = G // N_COLGRP
COL_CH = D // N_COLGRP
COL_ST = GRP_PER_COL * P
N_BU_SLOTS = 3
FFN_CHUNKS = ((0, 768), (768, 768), (1536, 768), (2304, 512))
L0_ROWS = 512
VMEM_LIMIT = 56 * 1024 * 1024

BF16 = jnp.bfloat16
F32 = jnp.float32


def _const_spec(shape):
    nd = len(shape)
    return pl.BlockSpec(shape, lambda i: (0,) * nd, pipeline_mode=pl.Buffered(1))


def _rms(x, g):
    return x * lax.rsqrt(jnp.mean(x * x, axis=-1, keepdims=True) + EPS) * g


def _dot(a, b):
    return jnp.dot(a, b, preferred_element_type=F32)


def _params(limit=VMEM_LIMIT):
    return pltpu.CompilerParams(dimension_semantics=("arbitrary",), vmem_limit_bytes=limit)


def _pw1_kernel(n_p, xp_ref, xs_ref, g_ref, w_ref, b_ref, vp_ref, xtp_ref, vs_ref, xts_ref):
    i = pl.program_id(0)

    def run(x3_ref, v_ref, xt_ref):
        xt = jnp.swapaxes(x3_ref[...], 0, 1).reshape(ROWS, D)
        xt_ref[...] = xt
        h = _rms(xt, g_ref[...]).astype(BF16)
        a = _dot(h, w_ref[...]) + b_ref[...]
        v_ref[...] = a[:, :D] * jax.nn.sigmoid(a[:, D:])

    @pl.when(i < n_p)
    def _():
        run(xp_ref, vp_ref, xtp_ref)

    @pl.when(i == n_p)
    def _():
        run(xs_ref, vs_ref, xts_ref)


def _pw1(x_prompt, x_sample, consts):
    n_seq, t_p, _ = x_prompt.shape
    n_s, t_s, _ = x_sample.shape
    tb = ROWS // n_seq
    n_p = t_p // tb
    assert n_s * t_s == ROWS and t_p % tb == 0
    p_in = pl.BlockSpec((n_seq, tb, D), lambda i: (0, jnp.minimum(i, n_p - 1), 0))
    s_in = pl.BlockSpec((n_s, t_s, D), lambda i: (0, 0, 0))
    p_out = pl.BlockSpec((ROWS, D), lambda i: (jnp.minimum(i, n_p - 1), 0))
    s_out = pl.BlockSpec((ROWS, D), lambda i: (0, 0))
    p_shape = jax.ShapeDtypeStruct((t_p * n_seq, D), F32)
    s_shape = jax.ShapeDtypeStruct((ROWS, D), F32)
    return pl.pallas_call(
        functools.partial(_pw1_kernel, n_p),
        grid=(n_p + 1,),
        in_specs=[p_in, s_in] + [_const_spec(c.shape) for c in consts],
        out_specs=[p_out, p_out, s_out, s_out],
        out_shape=[p_shape, p_shape, s_shape, s_shape],
        compiler_params=_params(),
        name="pw1",
    )(x_prompt, x_sample, *consts)


def _conv_blocks(full_ref, yc_ref, wdw_ref, bdw_ref, n_seq, n_t):
    t_blk = SUBLANES
    n_sb = n_seq // SUBLANES
    for lb in range(D // LANES):
        lanes = slice(lb * LANES, (lb + 1) * LANES)
        wk = [jnp.broadcast_to(wdw_ref[k:k + 1, lanes], (SUBLANES, LANES)) for k in range(KW)]
        bias = jnp.broadcast_to(bdw_ref[0:1, lanes], (SUBLANES, LANES))

        def body(u, carry):
            r0 = pl.multiple_of((u // n_sb) * (t_blk * n_seq) + (u % n_sb) * SUBLANES, SUBLANES)
            xs = [full_ref[pl.ds(r0 + j * n_seq, SUBLANES), lanes] for j in range(t_blk + HIST)]
            for t in range(t_blk):
                parts = [bias, None]
                for k in range(KW):
                    term = xs[t + k] * wk[k]
                    parts[k % 2] = term if parts[k % 2] is None else parts[k % 2] + term
                yc_ref[pl.ds(r0 + t * n_seq, SUBLANES), lanes] = parts[0] + parts[1]
            return carry

        lax.fori_loop(0, (n_t // t_blk) * n_sb, body, 0)


def _conv_sample_kernel(n_seq, n_t, full_ref, wdw_ref, bdw_ref, yc_ref):
    _conv_blocks(full_ref, yc_ref, wdw_ref, bdw_ref, n_seq, n_t)


def _conv_sample(full, n_seq, n_t, wdw, bdw):
    rows = n_seq * n_t
    assert full.shape == ((HIST + n_t) * n_seq, D)
    return pl.pallas_call(
        functools.partial(_conv_sample_kernel, n_seq, n_t),
        grid=(1,),
        in_specs=[_const_spec(full.shape), _const_spec(wdw.shape), _const_spec(bdw.shape)],
        out_specs=pl.BlockSpec((rows, D), lambda i: (0, 0)),
        out_shape=jax.ShapeDtypeStruct((rows, D), F32),
        compiler_params=_params(),
        name="conv_sample",
    )(full, wdw, bdw)


def _ffn(x, g_ref, wg_ref, wu_ref, wd_ref):
    h = _rms(x, g_ref[...]).astype(BF16)
    acc = None
    for c0, cw in FFN_CHUNKS:
        gate = _dot(h, wg_ref[:, c0:c0 + cw])
        up = _dot(h, wu_ref[:, c0:c0 + cw])
        act = (jax.nn.silu(gate) * up).astype(BF16)
        part = _dot(act, wd_ref[c0:c0 + cw, :])
        acc = part if acc is None else acc + part
    return x + acc


def _l0_kernel(n_pt, v_ref, x_ref, ycs_ref, xs_ref, wdw_ref, bdw_ref, lng_ref, lnb_ref, w2_ref,
               g_ref, wg_ref, wu_ref, wd_ref, op_ref, os_ref, full_ref, yc_ref, h_ref):
    s = pl.program_id(0)
    rows = L0_ROWS
    hist = HIST * SUBLANES
    t_blk = SUBLANES
    units = [(tb, lb) for tb in range(rows // (t_blk * SUBLANES)) for lb in range(D // LANES)]
    per_sec = len(units) // len(FFN_CHUNKS)

    def conv_units(sel):
        for tb, lb in sel:
            lanes = slice(lb * LANES, (lb + 1) * LANES)
            r0 = tb * t_blk * SUBLANES
            xs = [full_ref[r0 + j * SUBLANES:r0 + (j + 1) * SUBLANES, lanes]
                  for j in range(t_blk + HIST)]
            bias = jnp.broadcast_to(bdw_ref[0:1, lanes], (SUBLANES, LANES))
            for t in range(t_blk):
                parts = [bias, None]
                for k in range(KW):
                    wk = jnp.broadcast_to(wdw_ref[k:k + 1, lanes], (SUBLANES, LANES))
                    term = xs[t + k] * wk
                    parts[k % 2] = term if parts[k % 2] is None else parts[k % 2] + term
                r = r0 + t * SUBLANES
                yc_ref[r:r + SUBLANES, lanes] = parts[0] + parts[1]

    def prologue(yc, x, o_ref):
        mu = jnp.mean(yc, axis=-1, keepdims=True)
        xc = yc - mu
        var = jnp.mean(xc * xc, axis=-1, keepdims=True)
        yn = xc * lax.rsqrt(var + EPS) * lng_ref[...] + lnb_ref[...]
        y1 = x + _dot(jax.nn.silu(yn).astype(BF16), w2_ref[...])
        h_ref[...] = _rms(y1, g_ref[...]).astype(BF16)
        o_ref[...] = y1

    def ffn_chunk(ci, o_ref):
        c0, cw = FFN_CHUNKS[ci]
        h = h_ref[...]
        gate = _dot(h, wg_ref[:, c0:c0 + cw])
        up = _dot(h, wu_ref[:, c0:c0 + cw])
        act = (jax.nn.silu(gate) * up).astype(BF16)
        o_ref[...] += _dot(act, wd_ref[c0:c0 + cw, :])

    def load_tile():
        full_ref[hist:hist + rows, :] = v_ref[...]

    def keep_history():
        full_ref[0:hist, :] = full_ref[rows:rows + hist, :]

    @pl.when(s == 0)
    def _():
        full_ref[0:hist, :] = jnp.zeros((hist, D), F32)
        load_tile()
        _conv_blocks(full_ref, yc_ref, wdw_ref, bdw_ref, SUBLANES, rows // SUBLANES)
        keep_history()

    @pl.when(jnp.logical_and(s > 0, s < n_pt))
    def _():
        prologue(yc_ref[...], x_ref[...], op_ref)
        load_tile()
        for ci in range(len(FFN_CHUNKS)):
            def section(ci=ci):
                conv_units(units[ci * per_sec:(ci + 1) * per_sec])
                ffn_chunk(ci, op_ref)
            pl.when(s > 0)(section)
        keep_history()

    @pl.when(s == n_pt)
    def _():
        prologue(yc_ref[...], x_ref[...], op_ref)
        for ci in range(len(FFN_CHUNKS)):
            ffn_chunk(ci, op_ref)

    @pl.when(s > n_pt)
    def _():
        prologue(ycs_ref[...], xs_ref[...], os_ref)
        for ci in range(len(FFN_CHUNKS)):
            ffn_chunk(ci, os_ref)


def _layer0_tail(vp, xtp, ycs, xts, n_seq, consts):
    assert n_seq == SUBLANES, "prompt time step must be one 8-row tile"
    rows = L0_ROWS
    n_pt = vp.shape[0] // rows
    n_st = xts.shape[0] // rows
    tile = (rows, D)
    v_spec = pl.BlockSpec(tile, lambda s: (jnp.minimum(s, n_pt - 1), 0))
    p_spec = pl.BlockSpec(tile, lambda s: (jnp.clip(s - 1, 0, n_pt - 1), 0))
    s_idx = lambda s: (jnp.clip(s - n_pt - 1, 0, n_st - 1), 0)
    s_in = pl.BlockSpec(tile, s_idx, pipeline_mode=pl.Buffered(1))
    s_out = pl.BlockSpec(tile, s_idx)
    return pl.pallas_call(
        functools.partial(_l0_kernel, n_pt),
        grid=(n_pt + 1 + n_st,),
        in_specs=[v_spec, p_spec, s_in, s_in] + [_const_spec(c.shape) for c in consts],
        out_specs=[p_spec, s_out],
        out_shape=[jax.ShapeDtypeStruct(vp.shape, F32), jax.ShapeDtypeStruct(xts.shape, F32)],
        scratch_shapes=[pltpu.VMEM((HIST * n_seq + rows, D), F32), pltpu.VMEM((rows, D), F32),
                        pltpu.VMEM((rows, D), BF16)],
        compiler_params=_params(),
        name="layer0_tail",
    )(vp, xtp, ycs, xts, *consts)


def _ffn_final_kernel(n_p, xp_ref, xs_ref, g_ref, wg_ref, wu_ref, wd_ref, gf_ref,
                      op_ref, os_ref):
    i = pl.program_id(0)

    def run(x_ref, o_ref):
        y = _rms(_ffn(x_ref[...], g_ref, wg_ref, wu_ref, wd_ref), gf_ref[...])
        n, t, _ = o_ref.shape
        o_ref[...] = jnp.swapaxes(y.reshape(t, n, D), 0, 1)

    @pl.when(i < n_p)
    def _():
        run(xp_ref, op_ref)

    @pl.when(i == n_p)
    def _():
        run(xs_ref, os_ref)


def _ffn_final(xp, xs, n_seq, n_s, consts):
    n_p = xp.shape[0] // ROWS
    tb = ROWS // n_seq
    t_s = ROWS // n_s
    p_in = pl.BlockSpec((ROWS, D), lambda i: (jnp.minimum(i, n_p - 1), 0))
    s_in = pl.BlockSpec((ROWS, D), lambda i: (0, 0))
    p_out = pl.BlockSpec((n_seq, tb, D), lambda i: (0, jnp.minimum(i, n_p - 1), 0))
    s_out = pl.BlockSpec((n_s, t_s, D), lambda i: (0, 0, 0))
    return pl.pallas_call(
        functools.partial(_ffn_final_kernel, n_p),
        grid=(n_p + 1,),
        in_specs=[p_in, s_in] + [_const_spec(c.shape) for c in consts],
        out_specs=[p_out, s_out],
        out_shape=[jax.ShapeDtypeStruct((n_seq, n_p * tb, D), F32),
                   jax.ShapeDtypeStruct((n_s, t_s, D), F32)],
        compiler_params=_params(),
        name="ffn1",
    )(xp, xs, *consts)


def _s5_prep_kernel(lr_ref, li_ref, ldt_ref, lrx_ref, lix_ref, bre_ref, bim_ref,
                    are_ref, aim_ref, wre_ref, wim_ref):
    dt = jnp.exp(ldt_ref[...])

    def disc(lr, li):
        mag = jnp.exp(lr * dt)
        ang = li * dt
        ab_re = mag * jnp.cos(ang)
        ab_im = mag * jnp.sin(ang)
        nr = ab_re - 1.0
        den = lr * lr + li * li
        cf_re = (nr * lr + ab_im * li) / den
        cf_im = (ab_im * lr - nr * li) / den
        return ab_re, ab_im, cf_re, cf_im

    ab_re, ab_im, _, _ = disc(lr_ref[...], li_ref[...])
    are_ref[...] = ab_re
    aim_ref[...] = ab_im
    _, _, cf_re, cf_im = disc(lrx_ref[...], lix_ref[...])
    b_re = bre_ref[...]
    b_im = bim_ref[...]
    wre_ref[...] = cf_re * b_re - cf_im * b_im
    wim_ref[...] = cf_re * b_im + cf_im * b_re


def _s5_prep(lam_re, lam_im, log_dt, b_re, b_im):
    lrx = jnp.repeat(lam_re, C, axis=1)
    lix = jnp.repeat(lam_im, C, axis=1)
    full = lambda s: pl.BlockSpec(s, lambda: (0,) * len(s))
    ins = (lam_re, lam_im, log_dt.reshape(G, 1), lrx, lix,
           b_re.reshape(G, P * C), b_im.reshape(G, P * C))
    outs = [jax.ShapeDtypeStruct((G, P), F32)] * 2 + [jax.ShapeDtypeStruct((G, P * C), F32)] * 2
    return pl.pallas_call(
        _s5_prep_kernel,
        in_specs=[full(a.shape) for a in ins],
        out_specs=[full(o.shape) for o in outs],
        out_shape=outs,
        name="s5_prep",
    )(*ins)


def _block_diag(w):
    _, rows, b = w.shape
    shape = (rows, GRP_PER_COL * b)
    row_grp = lax.broadcasted_iota(jnp.int32, shape, 0) // (rows // GRP_PER_COL)
    col_grp = lax.broadcasted_iota(jnp.int32, shape, 1) // b
    return jnp.where(row_grp == col_grp, jnp.tile(w, (1, 1, GRP_PER_COL)), 0)


def _s5_kernel(carry, n_seq, n_t, x_ref, h0re_ref, h0im_ref, g_ref, wb_ref, wc_ref, are_ref,
               aim_ref, d_ref, wglu_ref, bglu_ref, o_ref, sre_ref, sim_ref, bu_ref, y_ref):
    rows = n_t * n_seq

    def init_state():
        sre_ref[...] = h0re_ref[...]
        sim_ref[...] = h0im_ref[...]

    if carry:
        pl.when(pl.program_id(0) == 0)(init_state)
    else:
        init_state()

    x = x_ref[...].reshape(rows, D)
    h = _rms(x, g_ref[...])
    hb = h.astype(BF16)

    def b_proj(j):
        bu_ref[j % N_BU_SLOTS] = _dot(hb[:, j * COL_CH:(j + 1) * COL_CH], wb_ref[j])

    def scan(j):
        buf = bu_ref.at[j % N_BU_SLOTS]
        st_lanes = slice(j * COL_ST, (j + 1) * COL_ST)
        a_re = jnp.broadcast_to(are_ref[:, st_lanes], (SUBLANES, COL_ST))
        a_im = jnp.broadcast_to(aim_ref[:, st_lanes], (SUBLANES, COL_ST))
        for sb in range(n_seq // SUBLANES):
            s0 = sb * SUBLANES
            sr = sre_ref[s0:s0 + SUBLANES, st_lanes]
            si = sim_ref[s0:s0 + SUBLANES, st_lanes]
            for t in range(n_t):
                r = t * n_seq + s0
                xr = buf[r:r + SUBLANES, 0:COL_ST]
                xi = buf[r:r + SUBLANES, COL_ST:2 * COL_ST]
                sr, si = a_re * sr - a_im * si + xr, a_re * si + a_im * sr + xi
                buf[r:r + SUBLANES, 0:COL_ST] = sr
                buf[r:r + SUBLANES, COL_ST:2 * COL_ST] = si
            sre_ref[s0:s0 + SUBLANES, st_lanes] = sr
            sim_ref[s0:s0 + SUBLANES, st_lanes] = si

    def c_proj(j):
        y_ref[:, j * COL_CH:(j + 1) * COL_CH] = _dot(
            bu_ref[j % N_BU_SLOTS].astype(BF16), wc_ref[j])

    b_proj(0)
    for j in range(N_COLGRP):
        if j + 1 < N_COLGRP:
            b_proj(j + 1)
        scan(j)
        c_proj(j)

    y = y_ref[...] + d_ref[...] * h
    yg = jax.nn.gelu(y).astype(BF16)
    z = _dot(yg, wglu_ref[...]) + bglu_ref[...]
    out = x + z[:, :D] * jax.nn.sigmoid(z[:, D:])
    o_ref[...] = out.reshape(n_t, n_seq, D)


def _s5_layer(x3, h0_re, h0_im, consts, carry, n_t, n_seq, name):
    t_all, n_all, _ = x3.shape
    if carry:
        assert n_seq == n_all
        grid = (t_all // n_t,)
        x_spec = pl.BlockSpec((n_t, n_seq, D), lambda i: (i, 0, 0))
        st_spec = pl.BlockSpec((n_seq, G * P), lambda i: (0, 0))
    else:
        assert n_t == t_all
        grid = (n_all // n_seq,)
        x_spec = pl.BlockSpec((n_t, n_seq, D), lambda i: (0, i, 0))
        st_spec = pl.BlockSpec((n_seq, G * P), lambda i: (i, 0))
    rows = n_t * n_seq
    st_shape = jax.ShapeDtypeStruct(h0_re.shape, F32)
    return pl.pallas_call(
        functools.partial(_s5_kernel, carry, n_seq, n_t),
        grid=grid,
        in_specs=[x_spec, st_spec, st_spec] + [_const_spec(c.shape) for c in consts],
        out_specs=[x_spec, st_spec, st_spec],
        out_shape=[jax.ShapeDtypeStruct(x3.shape, F32), st_shape, st_shape],
        scratch_shapes=[pltpu.VMEM((N_BU_SLOTS, rows, 2 * COL_ST), F32),
                        pltpu.VMEM((rows, D), F32)],
        compiler_params=_params(),
        name=name,
    )(x3, h0_re, h0_im, *consts)


def kernel(x_prompt, x_sample, cache_conv, state_ssm_re, state_ssm_im, norm_mix, norm_ffn, norm_final, conv_w_pw1, conv_b_pw1, conv_w_dw, conv_b_dw, conv_ln_g, conv_ln_b, conv_w_pw2, ssm_lam_re, ssm_lam_im, ssm_log_dt, ssm_b_re, ssm_b_im, ssm_c_re, ssm_c_im, ssm_d, ssm_w_glu, ssm_b_glu, ffn_w_gate, ffn_w_up, ffn_w_down):
    n_p, t_p, _ = x_prompt.shape
    n_s, t_s, _ = x_sample.shape
    row = lambda a: a.reshape(1, -1)

    vp, xtp, vs, xts = _pw1(
        x_prompt, x_sample,
        (row(norm_mix[0]), conv_w_pw1[0].astype(BF16), row(conv_b_pw1[0])))
    wdw, bdw = conv_w_dw[0], row(conv_b_dw[0])
    cache_tm = cache_conv[0].transpose(1, 0, 2).reshape(HIST * n_s, D)
    full_s = jnp.concatenate([cache_tm, vs], axis=0)
    ycs = _conv_sample(full_s, n_s, t_s, wdw, bdw)
    conv_p = vp[-HIST * n_p:].reshape(HIST, n_p, D).transpose(1, 0, 2)[None]
    conv_s = full_s[-HIST * n_s:].reshape(HIST, n_s, D).transpose(1, 0, 2)[None]
    l0_consts = (wdw, bdw, row(conv_ln_g[0]), row(conv_ln_b[0]), conv_w_pw2[0].astype(BF16),
                 row(norm_ffn[0]), ffn_w_gate[0].astype(BF16), ffn_w_up[0].astype(BF16),
                 ffn_w_down[0].astype(BF16))
    y2p, y2s = _layer0_tail(vp, xtp, ycs, xts, n_p, l0_consts)

    ab_re, ab_im, w_re, w_im = _s5_prep(ssm_lam_re[0], ssm_lam_im[0], ssm_log_dt[0],
                                        ssm_b_re[0], ssm_b_im[0])
    to_b = lambda w: _block_diag(w.reshape(N_COLGRP, GRP_PER_COL, P, C).transpose(0, 1, 3, 2)
                                 .reshape(N_COLGRP, COL_CH, P))
    wb = jnp.concatenate([to_b(w_re), to_b(w_im)], axis=-1).astype(BF16)
    to_c = lambda w: _block_diag(w.reshape(N_COLGRP, GRP_PER_COL, C, P).transpose(0, 1, 3, 2)
                                 .reshape(N_COLGRP, COL_ST, C))
    wc = jnp.concatenate([to_c(ssm_c_re[0]), to_c(-ssm_c_im[0])], axis=1).astype(BF16)
    s5_consts = (row(norm_mix[1]), wb, wc, row(ab_re), row(ab_im), row(ssm_d[0]),
                 ssm_w_glu[0].astype(BF16), row(ssm_b_glu[0]))

    zero_state = jnp.zeros((n_p, G * P), F32)
    y3p, ssm_p_re, ssm_p_im = _s5_layer(
        y2p.reshape(t_p, n_p, D), zero_state, zero_state,
        s5_consts, True, SSM_ROWS // n_p, n_p, "s5_prompt")
    y3s, ssm_s_re, ssm_s_im = _s5_layer(
        y2s.reshape(t_s, n_s, D), state_ssm_re[0].reshape(n_s, G * P),
        state_ssm_im[0].reshape(n_s, G * P),
        s5_consts, False, t_s, SSM_ROWS // t_s, "s5_sample")
    to_gp = lambda st: st.reshape(1, -1, G, P)

    ffn1 = (row(norm_ffn[1]), ffn_w_gate[1].astype(BF16), ffn_w_up[1].astype(BF16),
            ffn_w_down[1].astype(BF16), row(norm_final))
    y_prompt, y_sample = _ffn_final(y3p.reshape(t_p * n_p, D), y3s.reshape(t_s * n_s, D),
                                    n_p, n_s, ffn1)
    return (y_prompt, y_sample, conv_p, conv_s,
            to_gp(ssm_p_re), to_gp(ssm_p_im), to_gp(ssm_s_re), to_gp(ssm_s_im))
```

```python
import functools

import jax
import jax.numpy as jnp
from jax import lax
from jax.experimental import pallas as pl
from jax.experimental.pallas import tpu as pltpu

D = 1024
F = 2816
KW = 31
HIST = KW - 1
G = 64
P = 64
C = 16
EPS = 1e-6

SUBLANES = 8
LANES = 128

ROWS = 1024
SSM_ROWS = 512
N_COLGRP = 4
GRP_PER_COL = G // N_COLGRP
COL_CH = D // N_COLGRP
COL_ST = GRP_PER_COL * P
N_BU_SLOTS = 3
FFN_CHUNKS = ((0, 768), (768, 768), (1536, 768), (2304, 512))
L0_ROWS = 512
VMEM_LIMIT = 56 * 1024 * 1024

BF16 = jnp.bfloat16
F32 = jnp.float32


def _const_spec(shape):
    nd = len(shape)
    return pl.BlockSpec(shape, lambda i: (0,) * nd, pipeline_mode=pl.Buffered(1))


def _const_spec_of(c):
    if not isinstance(c, tuple):
        return _const_spec(c.shape)
    arr, layer = c
    nd = arr.ndim - 1
    return pl.BlockSpec((None,) + arr.shape[1:], lambda i: (layer,) + (0,) * nd,
                        pipeline_mode=pl.Buffered(1))


def _const_arr(c):
    return c[0] if isinstance(c, tuple) else c


def _rms(x, g):
    return x * lax.rsqrt(jnp.mean(x * x, axis=-1, keepdims=True) + EPS) * g


def _dot(a, b):
    return jnp.dot(a, b, preferred_element_type=F32)


def _params(limit=VMEM_LIMIT):
    return pltpu.CompilerParams(dimension_semantics=("arbitrary",), vmem_limit_bytes=limit)


def _pw1_kernel(n_p, xp_ref, xs_ref, g_ref, w_ref, b_ref, vp_ref, xtp_ref, vs_ref, xts_ref):
    i = pl.program_id(0)

    def run(x3_ref, v_ref, xt_ref):
        xt = jnp.swapaxes(x3_ref[...], 0, 1).reshape(ROWS, D)
        xt_ref[...] = xt
        h = _rms(xt, g_ref[...]).astype(BF16)
        a = _dot(h, w_ref[...]) + b_ref[...]
        v_ref[...] = a[:, :D] * jax.nn.sigmoid(a[:, D:])

    @pl.when(i < n_p)
    def _():
        run(xp_ref, vp_ref, xtp_ref)

    @pl.when(i == n_p)
    def _():
        run(xs_ref, vs_ref, xts_ref)


def _pw1(x_prompt, x_sample, consts):
    n_seq, t_p, _ = x_prompt.shape
    n_s, t_s, _ = x_sample.shape
    tb = ROWS // n_seq
    n_p = t_p // tb
    assert n_s * t_s == ROWS and t_p % tb == 0
    p_in = pl.BlockSpec((n_seq, tb, D), lambda i: (0, jnp.minimum(i, n_p - 1), 0))
    s_in = pl.BlockSpec((n_s, t_s, D), lambda i: (0, 0, 0))
    p_out = pl.BlockSpec((ROWS, D), lambda i: (jnp.minimum(i, n_p - 1), 0))
    s_out = pl.BlockSpec((ROWS, D), lambda i: (0, 0))
    p_shape = jax.ShapeDtypeStruct((t_p * n_seq, D), F32)
    s_shape = jax.ShapeDtypeStruct((ROWS, D), F32)
    return pl.pallas_call(
        functools.partial(_pw1_kernel, n_p),
        grid=(n_p + 1,),
        in_specs=[p_in, s_in] + [_const_spec_of(c) for c in consts],
        out_specs=[p_out, p_out, s_out, s_out],
        out_shape=[p_shape, p_shape, s_shape, s_shape],
        compiler_params=_params(),
        name="pw1",
    )(x_prompt, x_sample, *map(_const_arr, consts))


def _conv_blocks(full_ref, yc_ref, wdw_ref, bdw_ref, n_seq, n_t):
    t_blk = SUBLANES
    n_sb = n_seq // SUBLANES
    for lb in range(D // LANES):
        lanes = slice(lb * LANES, (lb + 1) * LANES)
        wk = [jnp.broadcast_to(wdw_ref[k:k + 1, lanes], (SUBLANES, LANES)) for k in range(KW)]
        bias = jnp.broadcast_to(bdw_ref[0:1, lanes], (SUBLANES, LANES))

        def body(u, carry):
            r0 = pl.multiple_of((u // n_sb) * (t_blk * n_seq) + (u % n_sb) * SUBLANES, SUBLANES)
            xs = [full_ref[pl.ds(r0 + j * n_seq, SUBLANES), lanes] for j in range(t_blk + HIST)]
            for t in range(t_blk):
                parts = [bias, None]
                for k in range(KW):
                    term = xs[t + k] * wk[k]
                    parts[k % 2] = term if parts[k % 2] is None else parts[k % 2] + term
                yc_ref[pl.ds(r0 + t * n_seq, SUBLANES), lanes] = parts[0] + parts[1]
            return carry

        lax.fori_loop(0, (n_t // t_blk) * n_sb, body, 0)


def _conv_sample_kernel(n_seq, n_t, cache_ref, vs_ref, wdw_ref, bdw_ref, yc_ref, full_ref):
    hist = HIST * n_seq
    full_ref[0:hist, :] = cache_ref[...]
    full_ref[hist:hist + n_t * n_seq, :] = vs_ref[...]
    _conv_blocks(full_ref, yc_ref, wdw_ref, bdw_ref, n_seq, n_t)


def _conv_sample(cache_tm, vs, n_seq, n_t, wdw, bdw):
    rows = n_seq * n_t
    assert cache_tm.shape == (HIST * n_seq, D) and vs.shape == (rows, D)
    return pl.pallas_call(
        functools.partial(_conv_sample_kernel, n_seq, n_t),
        grid=(1,),
        in_specs=[_const_spec(a.shape) for a in (cache_tm, vs, wdw, bdw)],
        out_specs=pl.BlockSpec((rows, D), lambda i: (0, 0)),
        out_shape=jax.ShapeDtypeStruct((rows, D), F32),
        scratch_shapes=[pltpu.VMEM(((HIST + n_t) * n_seq, D), F32)],
        compiler_params=_params(),
        name="conv_sample",
    )(cache_tm, vs, wdw, bdw)


def _ffn(x, g_ref, wg_ref, wu_ref, wd_ref):
    h = _rms(x, g_ref[...]).astype(BF16)
    acc = None
    for c0, cw in FFN_CHUNKS:
        gate = _dot(h, wg_ref[:, c0:c0 + cw])
        up = _dot(h, wu_ref[:, c0:c0 + cw])
        act = (jax.nn.silu(gate) * up).astype(BF16)
        part = _dot(act, wd_ref[c0:c0 + cw, :])
        acc = part if acc is None else acc + part
    return x + acc


def _l0_kernel(n_pt, v_ref, x_ref, ycs_ref, xs_ref, wdw_ref, bdw_ref, lng_ref, lnb_ref, w2_ref,
               g_ref, wg_ref, wu_ref, wd_ref, op_ref, os_ref, full_ref, yc_ref, h_ref):
    s = pl.program_id(0)
    rows = L0_ROWS
    hist = HIST * SUBLANES
    t_blk = SUBLANES
    units = [(tb, lb) for tb in range(rows // (t_blk * SUBLANES)) for lb in range(D // LANES)]
    per_sec = len(units) // len(FFN_CHUNKS)

    def conv_units(sel):
        for tb, lb in sel:
            lanes = slice(lb * LANES, (lb + 1) * LANES)
            r0 = tb * t_blk * SUBLANES
            xs = [full_ref[r0 + j * SUBLANES:r0 + (j + 1) * SUBLANES, lanes]
                  for j in range(t_blk + HIST)]
            bias = jnp.broadcast_to(bdw_ref[0:1, lanes], (SUBLANES, LANES))
            for t in range(t_blk):
                parts = [bias, None]
                for k in range(KW):
                    wk = jnp.broadcast_to(wdw_ref[k:k + 1, lanes], (SUBLANES, LANES))
                    term = xs[t + k] * wk
                    parts[k % 2] = term if parts[k % 2] is None else parts[k % 2] + term
                r = r0 + t * SUBLANES
                yc_ref[r:r + SUBLANES, lanes] = parts[0] + parts[1]

    def prologue(yc, x, o_ref):
        mu = jnp.mean(yc, axis=-1, keepdims=True)
        xc = yc - mu
        var = jnp.mean(xc * xc, axis=-1, keepdims=True)
        yn = xc * lax.rsqrt(var + EPS) * lng_ref[...] + lnb_ref[...]
        y1 = x + _dot(jax.nn.silu(yn).astype(BF16), w2_ref[...])
        h_ref[...] = _rms(y1, g_ref[...]).astype(BF16)
        o_ref[...] = y1

    def ffn_chunk(ci, o_ref):
        c0, cw = FFN_CHUNKS[ci]
        h = h_ref[...]
        gate = _dot(h, wg_ref[:, c0:c0 + cw])
        up = _dot(h, wu_ref[:, c0:c0 + cw])
        act = (jax.nn.silu(gate) * up).astype(BF16)
        o_ref[...] += _dot(act, wd_ref[c0:c0 + cw, :])

    def load_tile():
        full_ref[hist:hist + rows, :] = v_ref[...]

    def keep_history():
        full_ref[0:hist, :] = full_ref[rows:rows + hist, :]

    @pl.when(s == 0)
    def _():
        full_ref[0:hist, :] = jnp.zeros((hist, D), F32)
        load_tile()
        _conv_blocks(full_ref, yc_ref, wdw_ref, bdw_ref, SUBLANES, rows // SUBLANES)
        keep_history()

    @pl.when(jnp.logical_and(s > 0, s < n_pt))
    def _():
        prologue(yc_ref[...], x_ref[...], op_ref)
        load_tile()
        for ci in range(len(FFN_CHUNKS)):
            def section(ci=ci):
                conv_units(units[ci * per_sec:(ci + 1) * per_sec])
                ffn_chunk(ci, op_ref)
            pl.when(s > 0)(section)
        keep_history()

    @pl.when(s == n_pt)
    def _():
        prologue(yc_ref[...], x_ref[...], op_ref)
        for ci in range(len(FFN_CHUNKS)):
            ffn_chunk(ci, op_ref)

    @pl.when(s > n_pt)
    def _():
        prologue(ycs_ref[...], xs_ref[...], os_ref)
        for ci in range(len(FFN_CHUNKS)):
            ffn_chunk(ci, os_ref)


def _layer0_tail(vp, xtp, ycs, xts, n_seq, consts):
    assert n_seq == SUBLANES, "prompt time step must be one 8-row tile"
    rows = L0_ROWS
    n_pt = vp.shape[0] // rows
    n_st = xts.shape[0] // rows
    tile = (rows, D)
    v_spec = pl.BlockSpec(tile, lambda s: (jnp.minimum(s, n_pt - 1), 0))
    p_spec = pl.BlockSpec(tile, lambda s: (jnp.clip(s - 1, 0, n_pt - 1), 0))
    s_idx = lambda s: (jnp.clip(s - n_pt - 1, 0, n_st - 1), 0)
    s_in = pl.BlockSpec(tile, s_idx, pipeline_mode=pl.Buffered(1))
    s_out = pl.BlockSpec(tile, s_idx)
    return pl.pallas_call(
        functools.partial(_l0_kernel, n_pt),
        grid=(n_pt + 1 + n_st,),
        in_specs=[v_spec, p_spec, s_in, s_in] + [_const_spec_of(c) for c in consts],
        out_specs=[p_spec, s_out],
        out_shape=[jax.ShapeDtypeStruct(vp.shape, F32), jax.ShapeDtypeStruct(xts.shape, F32)],
        scratch_shapes=[pltpu.VMEM((HIST * n_seq + rows, D), F32), pltpu.VMEM((rows, D), F32),
                        pltpu.VMEM((rows, D), BF16)],
        compiler_params=_params(),
        name="layer0_tail",
    )(vp, xtp, ycs, xts, *map(_const_arr, consts))


def _ffn_final_kernel(n_p, xp_ref, xs_ref, g_ref, wg_ref, wu_ref, wd_ref, gf_ref,
                      op_ref, os_ref):
    i = pl.program_id(0)

    def run(x_ref, o_ref):
        y = _rms(_ffn(x_ref[...], g_ref, wg_ref, wu_ref, wd_ref), gf_ref[...])
        n, t, _ = o_ref.shape
        o_ref[...] = jnp.swapaxes(y.reshape(t, n, D), 0, 1)

    @pl.when(i < n_p)
    def _():
        run(xp_ref, op_ref)

    @pl.when(i == n_p)
    def _():
        run(xs_ref, os_ref)


def _ffn_final(xp, xs, n_seq, n_s, consts):
    n_p = xp.shape[0] // ROWS
    tb = ROWS // n_seq
    t_s = ROWS // n_s
    p_in = pl.BlockSpec((ROWS, D), lambda i: (jnp.minimum(i, n_p - 1), 0))
    s_in = pl.BlockSpec((ROWS, D), lambda i: (0, 0))
    p_out = pl.BlockSpec((n_seq, tb, D), lambda i: (0, jnp.minimum(i, n_p - 1), 0))
    s_out = pl.BlockSpec((n_s, t_s, D), lambda i: (0, 0, 0))
    return pl.pallas_call(
        functools.partial(_ffn_final_kernel, n_p),
        grid=(n_p + 1,),
        in_specs=[p_in, s_in] + [_const_spec_of(c) for c in consts],
        out_specs=[p_out, s_out],
        out_shape=[jax.ShapeDtypeStruct((n_seq, n_p * tb, D), F32),
                   jax.ShapeDtypeStruct((n_s, t_s, D), F32)],
        compiler_params=_params(),
        name="ffn1",
    )(xp, xs, *map(_const_arr, consts))


def _s5_prep_kernel(lr_ref, li_ref, ldt_ref, lrx_ref, lix_ref, bre_ref, bim_ref,
                    are_ref, aim_ref, wre_ref, wim_ref):
    dt = jnp.exp(ldt_ref[...])

    def disc(lr, li):
        mag = jnp.exp(lr * dt)
        ang = li * dt
        ab_re = mag * jnp.cos(ang)
        ab_im = mag * jnp.sin(ang)
        nr = ab_re - 1.0
        den = lr * lr + li * li
        cf_re = (nr * lr + ab_im * li) / den
        cf_im = (ab_im * lr - nr * li) / den
        return ab_re, ab_im, cf_re, cf_im

    ab_re, ab_im, _, _ = disc(lr_ref[...], li_ref[...])
    are_ref[...] = ab_re
    aim_ref[...] = ab_im
    _, _, cf_re, cf_im = disc(lrx_ref[...], lix_ref[...])
    b_re = bre_ref[...]
    b_im = bim_ref[...]
    wre_ref[...] = cf_re * b_re - cf_im * b_im
    wim_ref[...] = cf_re * b_im + cf_im * b_re


def _s5_prep(lam_re, lam_im, log_dt, b_re, b_im):
    lrx = jnp.repeat(lam_re, C, axis=1)
    lix = jnp.repeat(lam_im, C, axis=1)
    full = lambda s: pl.BlockSpec(s, lambda: (0,) * len(s))
    ins = (lam_re, lam_im, log_dt.reshape(G, 1), lrx, lix,
           b_re.reshape(G, P * C), b_im.reshape(G, P * C))
    outs = [jax.ShapeDtypeStruct((G, P), F32)] * 2 + [jax.ShapeDtypeStruct((G, P * C), F32)] * 2
    return pl.pallas_call(
        _s5_prep_kernel,
        in_specs=[full(a.shape) for a in ins],
        out_specs=[full(o.shape) for o in outs],
        out_shape=outs,
        name="s5_prep",
    )(*ins)


def _block_diag(w):
    _, rows, b = w.shape
    shape = (rows, GRP_PER_COL * b)
    row_grp = lax.broadcasted_iota(jnp.int32, shape, 0) // (rows // GRP_PER_COL)
    col_grp = lax.broadcasted_iota(jnp.int32, shape, 1) // b
    return jnp.where(row_grp == col_grp, jnp.tile(w, (1, 1, GRP_PER_COL)), 0)


def _s5_kernel(carry, n_seq, n_t, x_ref, h0re_ref, h0im_ref, g_ref, wb_ref, wc_ref, are_ref,
               aim_ref, d_ref, wglu_ref, bglu_ref, o_ref, sre_ref, sim_ref, bu_ref, y_ref):
    rows = n_t * n_seq

    def init_state():
        sre_ref[...] = h0re_ref[...]
        sim_ref[...] = h0im_ref[...]

    if carry:
        pl.when(pl.program_id(0) == 0)(init_state)
    else:
        init_state()

    x = x_ref[...].reshape(rows, D)
    h = _rms(x, g_ref[...])
    hb = h.astype(BF16)

    def b_proj(j):
        bu_ref[j % N_BU_SLOTS] = _dot(hb[:, j * COL_CH:(j + 1) * COL_CH], wb_ref[j])

    def scan(j):
        buf = bu_ref.at[j % N_BU_SLOTS]
        st_lanes = slice(j * COL_ST, (j + 1) * COL_ST)
        a_re = jnp.broadcast_to(are_ref[:, st_lanes], (SUBLANES, COL_ST))
        a_im = jnp.broadcast_to(aim_ref[:, st_lanes], (SUBLANES, COL_ST))
        for sb in range(n_seq // SUBLANES):
            s0 = sb * SUBLANES
            sr = sre_ref[s0:s0 + SUBLANES, st_lanes]
            si = sim_ref[s0:s0 + SUBLANES, st_lanes]
            for t in range(n_t):
                r = t * n_seq + s0
                xr = buf[r:r + SUBLANES, 0:COL_ST]
                xi = buf[r:r + SUBLANES, COL_ST:2 * COL_ST]
                sr, si = a_re * sr - a_im * si + xr, a_re * si + a_im * sr + xi
                buf[r:r + SUBLANES, 0:COL_ST] = sr
                buf[r:r + SUBLANES, COL_ST:2 * COL_ST] = si
            sre_ref[s0:s0 + SUBLANES, st_lanes] = sr
            sim_ref[s0:s0 + SUBLANES, st_lanes] = si

    def c_proj(j):
        y_ref[:, j * COL_CH:(j + 1) * COL_CH] = _dot(
            bu_ref[j % N_BU_SLOTS].astype(BF16), wc_ref[j])

    b_proj(0)
    for j in range(N_COLGRP):
        if j + 1 < N_COLGRP:
            b_proj(j + 1)
        scan(j)
        c_proj(j)

    y = y_ref[...] + d_ref[...] * h
    yg = jax.nn.gelu(y).astype(BF16)
    z = _dot(yg, wglu_ref[...]) + bglu_ref[...]
    out = x + z[:, :D] * jax.nn.sigmoid(z[:, D:])
    o_ref[...] = out.reshape(n_t, n_seq, D)


def _s5_layer(x3, h0_re, h0_im, consts, carry, n_t, n_seq, name):
    t_all, n_all, _ = x3.shape
    if carry:
        assert n_seq == n_all
        grid = (t_all // n_t,)
        x_spec = pl.BlockSpec((n_t, n_seq, D), lambda i: (i, 0, 0))
        st_spec = pl.BlockSpec((n_seq, G * P), lambda i: (0, 0))
    else:
        assert n_t == t_all
        grid = (n_all // n_seq,)
        x_spec = pl.BlockSpec((n_t, n_seq, D), lambda i: (0, i, 0))
        st_spec = pl.BlockSpec((n_seq, G * P), lambda i: (i, 0))
    rows = n_t * n_seq
    st_shape = jax.ShapeDtypeStruct(h0_re.shape, F32)
    return pl.pallas_call(
        functools.partial(_s5_kernel, carry, n_seq, n_t),
        grid=grid,
        in_specs=[x_spec, st_spec, st_spec] + [_const_spec_of(c) for c in consts],
        out_specs=[x_spec, st_spec, st_spec],
        out_shape=[jax.ShapeDtypeStruct(x3.shape, F32), st_shape, st_shape],
        scratch_shapes=[pltpu.VMEM((N_BU_SLOTS, rows, 2 * COL_ST), F32),
                        pltpu.VMEM((rows, D), F32)],
        compiler_params=_params(),
        name=name,
    )(x3, h0_re, h0_im, *map(_const_arr, consts))


def kernel(x_prompt, x_sample, cache_conv, state_ssm_re, state_ssm_im, norm_mix, norm_ffn, norm_final, conv_w_pw1, conv_b_pw1, conv_w_dw, conv_b_dw, conv_ln_g, conv_ln_b, conv_w_pw2, ssm_lam_re, ssm_lam_im, ssm_log_dt, ssm_b_re, ssm_b_im, ssm_c_re, ssm_c_im, ssm_d, ssm_w_glu, ssm_b_glu, ffn_w_gate, ffn_w_up, ffn_w_down):
    n_p, t_p, _ = x_prompt.shape
    n_s, t_s, _ = x_sample.shape
    row = lambda a: a.reshape(1, -1)

    vp, xtp, vs, xts = _pw1(
        x_prompt, x_sample,
        (row(norm_mix[0]), conv_w_pw1[0].astype(BF16), row(conv_b_pw1[0])))
    wdw, bdw = conv_w_dw[0], row(conv_b_dw[0])
    cache_tm = cache_conv[0].transpose(1, 0, 2).reshape(HIST * n_s, D)
    ycs = _conv_sample(cache_tm, vs, n_s, t_s, wdw, bdw)
    conv_p = vp[-HIST * n_p:].reshape(HIST, n_p, D).transpose(1, 0, 2)[None]
    conv_s = jnp.concatenate([cache_tm[t_s * n_s:], vs], axis=0)
    conv_s = conv_s.reshape(HIST, n_s, D).transpose(1, 0, 2)[None]
    wg16, wu16, wd16 = (w.astype(BF16) for w in (ffn_w_gate, ffn_w_up, ffn_w_down))
    l0_consts = (wdw, bdw, row(conv_ln_g[0]), row(conv_ln_b[0]), conv_w_pw2[0].astype(BF16),
                 row(norm_ffn[0]), (wg16, 0), (wu16, 0), (wd16, 0))
    y2p, y2s = _layer0_tail(vp, xtp, ycs, xts, n_p, l0_consts)

    ab_re, ab_im, w_re, w_im = _s5_prep(ssm_lam_re[0], ssm_lam_im[0], ssm_log_dt[0],
                                        ssm_b_re[0], ssm_b_im[0])
    to_b = lambda w: _block_diag(w.reshape(N_COLGRP, GRP_PER_COL, P, C).transpose(0, 1, 3, 2)
                                 .reshape(N_COLGRP, COL_CH, P))
    wb = jnp.concatenate([to_b(w_re), to_b(w_im)], axis=-1).astype(BF16)
    to_c = lambda w: _block_diag(w.reshape(N_COLGRP, GRP_PER_COL, C, P).transpose(0, 1, 3, 2)
                                 .reshape(N_COLGRP, COL_ST, C))
    wc = jnp.concatenate([to_c(ssm_c_re[0]), to_c(-ssm_c_im[0])], axis=1).astype(BF16)
    s5_consts = (row(norm_mix[1]), wb, wc, row(ab_re), row(ab_im), row(ssm_d[0]),
                 ssm_w_glu[0].astype(BF16), row(ssm_b_glu[0]))

    zero_state = jnp.zeros((n_p, G * P), F32)
    y3p, ssm_p_re, ssm_p_im = _s5_layer(
        y2p.reshape(t_p, n_p, D), zero_state, zero_state,
        s5_consts, True, SSM_ROWS // n_p, n_p, "s5_prompt")
    y3s, ssm_s_re, ssm_s_im = _s5_layer(
        y2s.reshape(t_s, n_s, D), state_ssm_re[0].reshape(n_s, G * P),
        state_ssm_im[0].reshape(n_s, G * P),
        s5_consts, False, t_s, SSM_ROWS // t_s, "s5_sample")
    to_gp = lambda st: st.reshape(1, -1, G, P)

    ffn1 = (row(norm_ffn[1]), (wg16, 1), (wu16, 1), (wd16, 1), row(norm_final))
    y_prompt, y_sample = _ffn_final(y3p.reshape(t_p * n_p, D), y3s.reshape(t_s * n_s, D),
                                    n_p, n_s, ffn1)
    return (y_prompt, y_sample, conv_p, conv_s,
            to_gp(ssm_p_re), to_gp(ssm_p_im), to_gp(ssm_s_re), to_gp(ssm_s_im))
```
